```python
import math
import jax, jax.numpy as jnp
from jax import lax
import numpy as np

D_MODEL = 1024
BATCH = 4
SEQ = 4096
DEPTH = 2

N_MOD = 6
EPS = 1e-6
SSM_EXPAND = 2
SSM_D_INNER = SSM_EXPAND * D_MODEL
SSM_HEAD_DIM = 64
SSM_HEADS = SSM_D_INNER // SSM_HEAD_DIM
SSM_GROUPS = 4
SSM_STATE = 128
SSM_CONV = 4
SSM_CHUNK = 128
SSM_CONV_DIM = SSM_D_INNER + 2 * SSM_GROUPS * SSM_STATE
ATT_HEAD_DIM = 64
ATT_KV_HEADS = D_MODEL // ATT_HEAD_DIM
ATT_PATTERNS = ((128, 1), (512, 4), (2048, 16))
ATT_GROUPS = len(ATT_PATTERNS)
ATT_Q_HEADS = ATT_GROUPS * ATT_KV_HEADS
ATT_WIDTH = ATT_KV_HEADS * ATT_HEAD_DIM
REL_BUCKETS = 32
REL_MAX_DIST = 2048
HY_SIZES = (SSM_D_INNER, SSM_CONV_DIM, SSM_HEADS, ATT_Q_HEADS * ATT_HEAD_DIM, ATT_WIDTH, ATT_WIDTH)
HY_SPLITS = tuple(int(v) for v in np.cumsum(HY_SIZES)[:-1])
HY_IN_DIM = sum(HY_SIZES)
HY_OUT_DIM = SSM_D_INNER + ATT_WIDTH
CONV_WIDTH = 31
FFN_HIDDEN = -(-8 * D_MODEL // (3 * 256)) * 256
N_EVEN = (DEPTH + 1) // 2
N_ODD = DEPTH // 2

kernel_name = "hybrid_ssd_dilated_conformer_block"


def rms_norm(x, g):
    xf = x.astype(jnp.float32)
    y = xf * lax.rsqrt(jnp.mean(xf * xf, -1, keepdims=True) + EPS)
    return (y * g.astype(jnp.float32)).astype(x.dtype)


def layer_norm(x, g, b):
    xf = x.astype(jnp.float32)
    mu = jnp.mean(xf, -1, keepdims=True)
    var = jnp.mean(jnp.square(xf - mu), -1, keepdims=True)
    y = (xf - mu) * lax.rsqrt(var + EPS)
    return (y * g.astype(jnp.float32) + b.astype(jnp.float32)).astype(x.dtype)


def causal_depthwise_conv(x, w, b):
    k = w.shape[0]
    out = lax.conv_general_dilated(
        x, w[:, None, :].astype(x.dtype), window_strides=(1,), padding=[(k - 1, 0)],
        dimension_numbers=('NWC', 'WIO', 'NWC'), feature_group_count=x.shape[-1])
    return out + b


def t5_bucket(dist):
    max_exact = REL_BUCKETS // 2
    n = jnp.maximum(dist, 1).astype(jnp.float32)
    large = max_exact + jnp.log(n / max_exact) / math.log(REL_MAX_DIST / max_exact) * (REL_BUCKETS - max_exact)
    large = jnp.minimum(large.astype(jnp.int32), REL_BUCKETS - 1)
    return jnp.where(dist < max_exact, dist, large)


def ssd_chunked(x, dt, A, Bm, Cm):
    f32 = jnp.float32
    b_, s, h, p = x.shape
    g, n = Bm.shape[2], Bm.shape[3]
    r, q = h // g, SSM_CHUNK
    nc = s // q
    x = x.astype(f32).reshape(b_, nc, q, g, r, p)
    dt = dt.astype(f32).reshape(b_, nc, q, g, r)
    Bm = Bm.astype(f32).reshape(b_, nc, q, g, n)
    Cm = Cm.astype(f32).reshape(b_, nc, q, g, n)
    a_cs = jnp.cumsum(dt * A.astype(f32).reshape(g, r), axis=2)
    xdt = x * dt[..., None]
    seg = a_cs[:, :, :, None] - a_cs[:, :, None, :]
    causal = jnp.tril(jnp.ones((q, q), bool))[:, :, None, None]
    decay = jnp.exp(jnp.where(causal, seg, -jnp.inf))
    cb = jnp.einsum('bclgn,bcsgn->bclsg', Cm, Bm)
    y_diag = jnp.einsum('bclsgr,bcsgrp->bclgrp', cb[..., None] * decay, xdt)
    decay_end = jnp.exp(a_cs[:, :, -1:] - a_cs)
    states = jnp.einsum('bcsgn,bcsgrp->bcgrpn', Bm, xdt * decay_end[..., None])
    chunk_decay = jnp.exp(a_cs[:, :, -1])

    def step(hs, inp):
        st, dec = inp
        return dec[..., None, None] * hs + st, hs

    h0 = jnp.zeros((b_, g, r, p, n), f32)
    _, prev = lax.scan(step, h0, (jnp.moveaxis(states, 1, 0), jnp.moveaxis(chunk_decay, 1, 0)))
    prev = jnp.moveaxis(prev, 0, 1)
    y_off = jnp.einsum('bclgn,bcgrpn->bclgrp', Cm, prev) * jnp.exp(a_cs)[..., None]
    return (y_diag + y_off).reshape(b_, s, h, p)


def dilated_branch(q, k, v, bias_tab, window, dil):
    b_, s, h, dh = q.shape
    blk = window // dil
    L = s // dil
    nb = -(-L // blk)
    lp = nb * blk

    def blocks(t):
        t = t.reshape(b_, L, dil, h, dh)
        t = jnp.pad(t, ((0, 0), (0, lp - L), (0, 0), (0, 0), (0, 0)))
        return t.reshape(b_, nb, blk, dil, h, dh)

    def band_keys(t):
        prev = jnp.pad(t, ((0, 0), (1, 0), (0, 0), (0, 0), (0, 0), (0, 0)))[:, :-1]
        return jnp.concatenate([prev, t], axis=2)

    qb = blocks(q)
    kw, vw = band_keys(blocks(k)), band_keys(blocks(v))
    i = jnp.arange(blk)[:, None]
    j = jnp.arange(2 * blk)[None, :]
    delta = blk + i - j
    band = (delta >= 0) & (delta <= blk)
    kpos = jnp.arange(nb)[:, None] * blk + jnp.arange(2 * blk)[None, :] - blk
    mask = band[None] & (kpos >= 0)[:, None, :]
    bias = jnp.transpose(bias_tab[t5_bucket(jnp.maximum(delta, 0) * dil)], (2, 0, 1)).astype(jnp.float32)
    sc = jnp.einsum('bnirhd,bnjrhd->bnrhij', qb, kw).astype(jnp.float32) * (dh ** -0.5) + bias
    sc = jnp.where(mask[None, :, None, None], sc, -jnp.inf)
    m = jnp.max(sc, -1, keepdims=True)
    pr = jnp.exp(sc - m)
    l = jnp.sum(pr, -1, keepdims=True)
    o = jnp.einsum('bnrhij,bnjrhd->bnirhd', (pr / l).astype(v.dtype), vw)
    lse = jnp.transpose((m + jnp.log(l))[..., 0], (0, 1, 4, 2, 3))
    lse = lse.reshape(b_, lp, dil, h)[:, :L].reshape(b_, s, h)
    o = o.reshape(b_, lp, dil, h, dh)[:, :L].reshape(b_, s, h, dh)
    return o, lse


def dilated_attention(q, k, v, rel_table):
    b_, s, _, h, dh = q.shape
    outs, lses = [], []
    for gi, (w, d) in enumerate(ATT_PATTERNS):
        o, lse = dilated_branch(q[:, :, gi], k, v, rel_table[:, gi * h:(gi + 1) * h], w, d)
        outs.append(o)
        lses.append(lse)
    wgt = jax.nn.softmax(jnp.stack(lses, 0), axis=0)
    o = jnp.einsum('gbsh,gbshd->bshd', wgt.astype(outs[0].dtype), jnp.stack(outs, 0))
    return o.reshape(b_, s, h * dh)


def hybrid_mixer(h, w_in, conv_w, conv_b, dt_bias, a_log, d_skip, ssm_norm_g, w_out, rel_table):
    b_, s, _ = h.shape
    z, xbc, dt_raw, q, k, v = jnp.split(h @ w_in, HY_SPLITS, axis=-1)
    xbc = jax.nn.silu(causal_depthwise_conv(xbc, conv_w, conv_b))
    xs, bm, cm = jnp.split(xbc, (SSM_D_INNER, SSM_D_INNER + SSM_GROUPS * SSM_STATE), axis=-1)
    xs = xs.reshape(b_, s, SSM_HEADS, SSM_HEAD_DIM)
    dt = jax.nn.softplus((dt_raw + dt_bias).astype(jnp.float32))
    A = -jnp.exp(a_log.astype(jnp.float32))
    y = ssd_chunked(xs, dt, A, bm.reshape(b_, s, SSM_GROUPS, SSM_STATE), cm.reshape(b_, s, SSM_GROUPS, SSM_STATE))
    y = y + d_skip.astype(jnp.float32)[:, None] * xs.astype(jnp.float32)
    y = y.reshape(b_, s, SSM_D_INNER).astype(h.dtype)
    y = rms_norm(y * jax.nn.silu(z), ssm_norm_g)
    att = dilated_attention(q.reshape(b_, s, ATT_GROUPS, ATT_KV_HEADS, ATT_HEAD_DIM),
                            k.reshape(b_, s, ATT_KV_HEADS, ATT_HEAD_DIM),
                            v.reshape(b_, s, ATT_KV_HEADS, ATT_HEAD_DIM), rel_table)
    return jnp.concatenate([y, att.astype(y.dtype)], axis=-1) @ w_out


def conformer_conv(h, w1, b1, w_dw, b_dw, ln_g, ln_b, w2, b2):
    a, gt = jnp.split(h @ w1 + b1, 2, axis=-1)
    u = a * jax.nn.sigmoid(gt)
    u = causal_depthwise_conv(u, w_dw, b_dw)
    u = jax.nn.silu(layer_norm(u, ln_g, ln_b))
    return u @ w2 + b2


def swiglu(h, wg, wu, wd):
    return (jax.nn.silu(h @ wg) * (h @ wu)) @ wd


def setup_inputs(seed: int = 0) -> dict:
    key = jax.random.key(seed)
    ks = iter(jax.random.split(key, 40))
    f32 = jnp.float32

    def nrm(shape, scale=1.0):
        return jax.random.normal(next(ks), shape, f32) * scale

    D = D_MODEL
    dt0 = jnp.exp(jax.random.uniform(next(ks), (N_EVEN, SSM_HEADS), f32, math.log(1e-3), math.log(1e-1)))
    return {
        "x": nrm((BATCH, SEQ, D)),
        "c": nrm((BATCH, D)),
        "ada_w": nrm((DEPTH, D, N_MOD * D), 0.5 * D ** -0.5),
        "ada_b": nrm((DEPTH, N_MOD * D), 0.02),
        "norm_mix_g": 1.0 + nrm((DEPTH, D), 0.05),
        "norm_ffn_g": 1.0 + nrm((DEPTH, D), 0.05),
        "hy_w_in": nrm((N_EVEN, D, HY_IN_DIM), D ** -0.5),
        "hy_conv_w": nrm((N_EVEN, SSM_CONV, SSM_CONV_DIM), SSM_CONV ** -0.5),
        "hy_conv_b": nrm((N_EVEN, SSM_CONV_DIM), 0.02),
        "hy_dt_bias": dt0 + jnp.log(-jnp.expm1(-dt0)),
        "hy_a_log": jnp.log(jax.random.uniform(next(ks), (N_EVEN, SSM_HEADS), f32, 1.0, 16.0)),
        "hy_d_skip": 1.0 + nrm((N_EVEN, SSM_HEADS), 0.1),
        "hy_ssm_norm_g": 1.0 + nrm((N_EVEN, SSM_D_INNER), 0.05),
        "hy_w_out": nrm((N_EVEN, HY_OUT_DIM, D), HY_OUT_DIM ** -0.5),
        "rel_table": nrm((REL_BUCKETS, ATT_Q_HEADS), 0.2),
        "cv_w_pw1": nrm((N_ODD, D, 2 * D), D ** -0.5),
        "cv_b_pw1": nrm((N_ODD, 2 * D), 0.02),
        "cv_w_dw": nrm((N_ODD, CONV_WIDTH, D), CONV_WIDTH ** -0.5),
        "cv_b_dw": nrm((N_ODD, D), 0.02),
        "cv_ln_g": 1.0 + nrm((N_ODD, D), 0.05),
        "cv_ln_b": nrm((N_ODD, D), 0.02),
        "cv_w_pw2": nrm((N_ODD, D, D), D ** -0.5),
        "cv_b_pw2": nrm((N_ODD, D), 0.02),
        "ffn_w_gate": nrm((DEPTH, D, FFN_HIDDEN), D ** -0.5),
        "ffn_w_up": nrm((DEPTH, D, FFN_HIDDEN), D ** -0.5),
        "ffn_w_down": nrm((DEPTH, FFN_HIDDEN, D), FFN_HIDDEN ** -0.5),
        "final_norm_g": 1.0 + nrm((D,), 0.05),
    }


def reference(x, c, ada_w, ada_b, norm_mix_g, norm_ffn_g, hy_w_in, hy_conv_w, hy_conv_b, hy_dt_bias,
              hy_a_log, hy_d_skip, hy_ssm_norm_g, hy_w_out, rel_table, cv_w_pw1, cv_b_pw1, cv_w_dw,
              cv_b_dw, cv_ln_g, cv_ln_b, cv_w_pw2, cv_b_pw2, ffn_w_gate, ffn_w_up, ffn_w_down,
              final_norm_g):
    cs = jax.nn.silu(c)
    for i in range(DEPTH):
        mod = cs @ ada_w[i] + ada_b[i]
        sh1, sc1, g1, sh2, sc2, g2 = [m[:, None, :] for m in jnp.split(mod, N_MOD, axis=-1)]
        h = rms_norm(x, norm_mix_g[i]) * (1 + sc1) + sh1
        j = i // 2
        if i % 2 == 0:
            mix = hybrid_mixer(h, hy_w_in[j], hy_conv_w[j], hy_conv_b[j], hy_dt_bias[j], hy_a_log[j],
                               hy_d_skip[j], hy_ssm_norm_g[j], hy_w_out[j], rel_table)
        else:
            mix = conformer_conv(h, cv_w_pw1[j], cv_b_pw1[j], cv_w_dw[j], cv_b_dw[j], cv_ln_g[j],
                                 cv_ln_b[j], cv_w_pw2[j], cv_b_pw2[j])
        x = x + g1 * mix
        h = rms_norm(x, norm_ffn_g[i]) * (1 + sc2) + sh2
        x = x + g2 * swiglu(h, ffn_w_gate[i], ffn_w_up[i], ffn_w_down[i])
    return rms_norm(x, final_norm_g)
```

```python
import functools
import math

import numpy as np
import jax
import jax.numpy as jnp
from jax import lax
from jax.experimental import pallas as pl
from jax.experimental.pallas import tpu as pltpu

F32 = jnp.float32
BF16 = jnp.bfloat16

D_MODEL = 1024
BATCH = 4
SEQ = 4096
DEPTH = 2
N_MOD = 6
EPS = 1e-6
SSM_D_INNER = 2048
SSM_HEAD_DIM = 64
SSM_HEADS = 32
SSM_GROUPS = 4
SSM_HEADS_PER_GROUP = SSM_HEADS // SSM_GROUPS
SSM_STATE = 128
SSM_CONV = 4
SSM_CHUNK = 128
SSM_BC = 2 * SSM_GROUPS * SSM_STATE
ATT_HEAD_DIM = 64
ATT_KV_HEADS = 16
ATT_PATTERNS = ((128, 1), (512, 4), (2048, 16))
ATT_GROUPS = len(ATT_PATTERNS)
ATT_WIDTH = ATT_KV_HEADS * ATT_HEAD_DIM
ATT_BLK = 128
ATT_PAIRS = ATT_KV_HEADS // 2
REL_BUCKETS = 32
REL_MAX_DIST = 2048
CONV_WIDTH = 31
CONV_HALO = 32
FFN_HIDDEN = 2816

COL_Z = 0
COL_X = SSM_D_INNER
COL_BC = COL_X + SSM_D_INNER
COL_Q = COL_BC + SSM_BC
COL_K = COL_Q + ATT_GROUPS * ATT_WIDTH
COL_V = COL_K + ATT_WIDTH
IN_MAIN = COL_V + ATT_WIDTH

LANE = 128
NEG = -1e30
VMEM_LIMIT = 56 * 1024 * 1024


def _params(*sem):
    return pltpu.CompilerParams(dimension_semantics=sem, vmem_limit_bytes=VMEM_LIMIT)


def _sigmoid(x):
    return 1.0 / (1.0 + jnp.exp(-x))


def _silu(x):
    return x * _sigmoid(x)


def _norm_mod(x, g, sc, sh):
    y = x * lax.rsqrt(jnp.mean(x * x, -1, keepdims=True) + EPS)
    return (y * g) * (1.0 + sc) + sh


def _split3(v):
    hi = v.astype(BF16)
    r1 = v - hi.astype(F32)
    mid = r1.astype(BF16)
    lo = (r1 - mid.astype(F32)).astype(BF16)
    return hi, mid, lo


def _dot(a, b):
    return jnp.dot(a, b, preferred_element_type=F32)


def _dot_nt(a, b):
    return lax.dot_general(a, b, (((1,), (1,)), ((), ())), preferred_element_type=F32)


def _dot_tn(a, b):
    return lax.dot_general(a, b, (((0,), (0,)), ((), ())), preferred_element_type=F32)


def _expand(v, e, terms):
    parts = _split3(v)[:terms]
    out = _dot(parts[0], e)
    for p in parts[1:]:
        out = out + _dot(p, e)
    return out


def _mod_kernel(c_ref, w_ref, b_ref, o_ref):
    cs = _silu(c_ref[...])
    hi, mid, lo = _split3(cs)
    w = w_ref[0]
    whi = w.astype(BF16)
    wlo = (w - whi.astype(F32)).astype(BF16)
    acc = _dot(hi, whi) + _dot(mid, whi) + _dot(hi, wlo) + _dot(lo, whi) + _dot(mid, wlo)
    o_ref[0] = acc + b_ref[0]


def _modulation(c, ada_w, ada_b):
    tn = 1024
    n = N_MOD * D_MODEL
    c8 = jnp.concatenate([c, jnp.zeros((8 - BATCH, D_MODEL), F32)], axis=0)
    return pl.pallas_call(
        _mod_kernel,
        grid=(DEPTH, n // tn),
        in_specs=[
            pl.BlockSpec((8, D_MODEL), lambda i, j: (0, 0)),
            pl.BlockSpec((1, D_MODEL, tn), lambda i, j: (i, 0, j)),
            pl.BlockSpec((1, 1, tn), lambda i, j: (i, 0, j)),
        ],
        out_specs=pl.BlockSpec((1, 8, tn), lambda i, j: (i, 0, j)),
        out_shape=jax.ShapeDtypeStruct((DEPTH, 8, n), F32),
        compiler_params=_params("parallel", "parallel"),
        name="adaln_mod",
    )(c8, ada_w, ada_b.reshape(DEPTH, 1, n))


def _inproj_kernel(x_ref, g_ref, sc_ref, sh_ref, w_ref, wdt_ref, o_ref, dt_ref, h_ref):
    @pl.when(pl.program_id(1) == 0)
    def _():
        h = _norm_mod(x_ref[...], g_ref[...], sc_ref[0], sh_ref[0]).astype(BF16)
        h_ref[...] = h
        dt_ref[...] = _dot(h, wdt_ref[...])

    o_ref[...] = _dot(h_ref[...], w_ref[...]).astype(o_ref.dtype)


def _inproj(x2, g, sc, sh, w_main, w_dt):
    m = x2.shape[0]
    tm, tn = 1024, 1024
    per_b = SEQ // tm
    return pl.pallas_call(
        _inproj_kernel,
        grid=(m // tm, IN_MAIN // tn),
        in_specs=[
            pl.BlockSpec((tm, D_MODEL), lambda i, j: (i, 0)),
            pl.BlockSpec((1, D_MODEL), lambda i, j: (0, 0)),
            pl.BlockSpec((1, 1, D_MODEL), lambda i, j: (i // per_b, 0, 0)),
            pl.BlockSpec((1, 1, D_MODEL), lambda i, j: (i // per_b, 0, 0)),
            pl.BlockSpec((D_MODEL, tn), lambda i, j: (0, j)),
            pl.BlockSpec((D_MODEL, LANE), lambda i, j: (0, 0)),
        ],
        out_specs=[
            pl.BlockSpec((tm, tn), lambda i, j: (i, j)),
            pl.BlockSpec((tm, LANE), lambda i, j: (i, 0)),
        ],
        out_shape=[
            jax.ShapeDtypeStruct((m, IN_MAIN), BF16),
            jax.ShapeDtypeStruct((m, LANE), F32),
        ],
        scratch_shapes=[pltpu.VMEM((tm, D_MODEL), BF16)],
        compiler_params=_params("parallel", "arbitrary"),
        name="inproj",
    )(x2, g, sc, sh, w_main, w_dt)


def _ssd_kernel(z_ref, xs_ref, bc_ref, dt_ref, cw_ref, cb_ref, dtb_ref, alog_ref, dskip_ref, ng_ref,
                tril_ref, e_ref, o_ref, xbuf, state, yacc):
    q = SSM_CHUNK
    width = SSM_D_INNER + SSM_BC

    @pl.when(pl.program_id(1) == 0)
    def _():
        xbuf[0:8, :] = jnp.zeros((8, width), F32)
        state[...] = jnp.zeros(state.shape, F32)

    xbuf[8:8 + q, 0:SSM_D_INNER] = xs_ref[...].astype(F32)
    xbuf[8:8 + q, SSM_D_INNER:width] = bc_ref[...].astype(F32)
    acc = cb_ref[...] + cw_ref[0:1, :] * xbuf[pl.ds(8 - (SSM_CONV - 1), q), :]
    for k in range(1, SSM_CONV):
        acc = acc + cw_ref[k:k + 1, :] * xbuf[pl.ds(8 - (SSM_CONV - 1) + k, q), :]
    xbuf[0:8, :] = xbuf[q:q + 8, :]
    xc = _silu(acc)
    xs = xc[:, 0:SSM_D_INNER]

    dtr = dt_ref[...] + dtb_ref[...]
    dt = jnp.maximum(dtr, 0.0) + jnp.log(1.0 + jnp.exp(-jnp.abs(dtr)))
    a = dt * (-jnp.exp(alog_ref[...]))
    tril = tril_ref[...]
    a_hi, a_mid, a_lo = _split3(a)
    a_cs = _dot(tril, a_hi) + _dot(tril, a_mid) + _dot(tril, a_lo)
    a_cs_t = a_cs.T
    e = e_ref[...]
    a_cs_e = _expand(a_cs, e, 3)
    dt_e = _expand(dt, e, 2)
    last_e = a_cs_e[q - 1:q, :]
    xdt = xs * dt_e
    xdt_b = xdt.astype(BF16)
    xdtd_b = (xdt * jnp.exp(last_e - a_cs_e)).astype(BF16)
    exp_acs_e = jnp.exp(a_cs_e)
    chunk_dec_e = jnp.exp(last_e)

    row_i = lax.broadcasted_iota(jnp.int32, (q, q), 0)
    col_i = lax.broadcasted_iota(jnp.int32, (q, q), 1)
    causal = row_i >= col_i
    first_half = col_i < SSM_HEAD_DIM

    gw = SSM_HEADS_PER_GROUP * SSM_HEAD_DIM
    for g in range(SSM_GROUPS):
        bg = xc[:, SSM_D_INNER + g * SSM_STATE:SSM_D_INNER + (g + 1) * SSM_STATE].astype(BF16)
        c0 = SSM_D_INNER + SSM_GROUPS * SSM_STATE + g * SSM_STATE
        cg = xc[:, c0:c0 + SSM_STATE].astype(BF16)
        cb = _dot_nt(cg, bg)
        sg = state[:, g * gw:(g + 1) * gw]
        y_off = _dot(cg, sg.astype(BF16)) * exp_acs_e[:, g * gw:(g + 1) * gw]
        for pr in range(SSM_HEADS_PER_GROUP // 2):
            lo_col = g * gw + pr * LANE
            rhs = xdt_b[:, lo_col:lo_col + LANE]
            halves = []
            for hh in range(2):
                h = g * SSM_HEADS_PER_GROUP + 2 * pr + hh
                seg = a_cs[:, h:h + 1] - a_cs_t[h:h + 1, :]
                dec = jnp.exp(jnp.where(causal, seg, NEG))
                halves.append(_dot((cb * dec).astype(BF16), rhs))
            yd = jnp.where(first_half, halves[0], halves[1])
            yacc[:, lo_col:lo_col + LANE] = yd + y_off[:, pr * LANE:(pr + 1) * LANE]
        s_new = _dot_tn(bg, xdtd_b[:, g * gw:(g + 1) * gw])
        state[:, g * gw:(g + 1) * gw] = chunk_dec_e[:, g * gw:(g + 1) * gw] * sg + s_new

    y = yacc[...] + dskip_ref[...] * xs
    u = y * _silu(z_ref[...].astype(F32))
    o = u * lax.rsqrt(jnp.mean(u * u, -1, keepdims=True) + EPS) * ng_ref[...]
    o_ref[...] = o.astype(o_ref.dtype)


def _ssd(zx, dt_raw, conv_w, conv_b, dt_bias, a_log, d_skip, norm_g):
    m = zx.shape[0]
    q = SSM_CHUNK
    nc = SEQ // q
    width = SSM_D_INNER + SSM_BC
    pad = LANE - SSM_HEADS
    dtb = jnp.pad(dt_bias, (0, pad)).reshape(1, LANE)
    alog = jnp.pad(a_log, (0, pad)).reshape(1, LANE)
    dskip_e = jnp.repeat(d_skip, SSM_HEAD_DIM).reshape(1, SSM_D_INNER)
    tril = jnp.asarray(np.tril(np.ones((q, q), np.float32)), BF16)
    e_np = np.zeros((LANE, SSM_D_INNER), np.float32)
    for h in range(SSM_HEADS):
        e_np[h, h * SSM_HEAD_DIM:(h + 1) * SSM_HEAD_DIM] = 1.0
    e = jnp.asarray(e_np, BF16)
    row = lambda b, c: (b * nc + c, 0)
    const = lambda b, c: (0, 0)
    return pl.pallas_call(
        _ssd_kernel,
        grid=(BATCH, nc),
        in_specs=[
            pl.BlockSpec((q, SSM_D_INNER), lambda b, c: (b * nc + c, COL_Z // SSM_D_INNER)),
            pl.BlockSpec((q, SSM_D_INNER), lambda b, c: (b * nc + c, COL_X // SSM_D_INNER)),
            pl.BlockSpec((q, SSM_BC), lambda b, c: (b * nc + c, COL_BC // SSM_BC)),
            pl.BlockSpec((q, LANE), row),
            pl.BlockSpec((SSM_CONV, width), const),
            pl.BlockSpec((1, width), const),
            pl.BlockSpec((1, LANE), const),
            pl.BlockSpec((1, LANE), const),
            pl.BlockSpec((1, SSM_D_INNER), const),
            pl.BlockSpec((1, SSM_D_INNER), const),
            pl.BlockSpec((q, q), const),
            pl.BlockSpec((LANE, SSM_D_INNER), const),
        ],
        out_specs=pl.BlockSpec((q, SSM_D_INNER), row),
        out_shape=jax.ShapeDtypeStruct((m, SSM_D_INNER), BF16),
        scratch_shapes=[
            pltpu.VMEM((q + 8, width), F32),
            pltpu.VMEM((SSM_STATE, SSM_D_INNER), F32),
            pltpu.VMEM((q, SSM_D_INNER), F32),
        ],
        compiler_params=_params("parallel", "arbitrary"),
        name="ssd",
    )(zx, zx, zx, dt_raw, conv_w, conv_b.reshape(1, width), dtb, alog, dskip_e,
      norm_g.reshape(1, SSM_D_INNER), tril, e)


def _t5_bucket_np(dist):
    max_exact = REL_BUCKETS // 2
    n = np.maximum(dist, 1).astype(np.float32)
    large = max_exact + (np.log(n / np.float32(max_exact)) / np.float32(math.log(REL_MAX_DIST / max_exact))
                         * np.float32(REL_BUCKETS - max_exact))
    large = np.minimum(large.astype(np.int32), REL_BUCKETS - 1)
    return np.where(dist < max_exact, dist, large).astype(np.int32)


def _bucket_table():
    i = np.arange(ATT_BLK)[:, None]
    j = np.arange(2 * ATT_BLK)[None, :]
    delta = ATT_BLK + i - j
    band = (delta >= 0) & (delta <= ATT_BLK)
    out = []
    for _, dil in ATT_PATTERNS:
        out.append(np.where(band, _t5_bucket_np(np.maximum(delta, 0) * dil), -1))
    return np.stack(out).astype(np.int32)


def _bias_kernel(tab_ref, bucket_ref, o_ref):
    g = pl.program_id(0)
    p = pl.program_id(1)
    bucket = bucket_ref[0]
    for hh in range(2):
        head = g * ATT_KV_HEADS + 2 * p + hh
        acc = jnp.full((ATT_BLK, 2 * ATT_BLK), NEG, F32)
        for b in range(REL_BUCKETS):
            acc = jnp.where(bucket == b, tab_ref[b, head], acc)
        o_ref[0, 0, hh * ATT_BLK:(hh + 1) * ATT_BLK, :] = acc


def _rel_bias(rel_table):
    buckets = jnp.asarray(_bucket_table())
    return pl.pallas_call(
        _bias_kernel,
        grid=(ATT_GROUPS, ATT_PAIRS),
        in_specs=[
            pl.BlockSpec(memory_space=pltpu.SMEM),
            pl.BlockSpec((1, ATT_BLK, 2 * ATT_BLK), lambda g, p: (g, 0, 0)),
        ],
        out_specs=pl.BlockSpec((1, 1, 2 * ATT_BLK, 2 * ATT_BLK), lambda g, p: (g, p, 0, 0)),
        out_shape=jax.ShapeDtypeStruct((ATT_GROUPS, ATT_PAIRS, 2 * ATT_BLK, 2 * ATT_BLK), F32),
        compiler_params=_params("parallel", "parallel"),
        name="rel_bias",
    )(rel_table, buckets)


def _attn_kernel(q_ref, kp_ref, kc_ref, vp_ref, vc_ref, bias_ref, o_ref, lse_ref):
    blk = ATT_BLK
    n = pl.program_id(2)
    col = lax.broadcasted_iota(jnp.int32, (1, 2 * blk), 1)
    pen = jnp.where(col < blk, NEG, 0.0) * (n == 0).astype(F32)
    lane = lax.broadcasted_iota(jnp.int32, (blk, LANE), 1)
    first = lane < ATT_HEAD_DIM
    lse_acc = jnp.zeros((blk, LANE), F32)
    for p in range(ATT_PAIRS):
        sl = slice(p * LANE, (p + 1) * LANE)
        q = q_ref[0, :, sl]
        zero = jnp.zeros_like(q)
        qq = jnp.concatenate([jnp.where(first, q, zero), jnp.where(first, zero, q)], axis=0)
        kk = jnp.concatenate([kp_ref[0, :, sl], kc_ref[0, :, sl]], axis=0)
        vv = jnp.concatenate([vp_ref[0, :, sl], vc_ref[0, :, sl]], axis=0)
        s = _dot_nt(qq, kk) + bias_ref[0, p] + pen
        m = jnp.max(s, -1, keepdims=True)
        ex = jnp.exp(s - m)
        l = jnp.sum(ex, -1, keepdims=True)
        o2 = _dot(ex.astype(BF16), vv) / l
        o_ref[0, :, sl] = jnp.where(first, o2[0:blk], o2[blk:2 * blk]).astype(o_ref.dtype)
        lse = m + jnp.log(l)
        lse_acc = jnp.where(lane == 2 * p, lse[0:blk], lse_acc)
        lse_acc = jnp.where(lane == 2 * p + 1, lse[blk:2 * blk], lse_acc)
    lse_ref[0] = lse_acc


def _attention(zx, bias, gi):
    dil = ATT_PATTERNS[gi][1]
    blk = ATT_BLK
    length = SEQ // dil
    nb = length // blk
    zv = zx.reshape(BATCH, length, dil * IN_MAIN)
    cw = IN_MAIN // ATT_WIDTH
    qc, kc, vc = (COL_Q + gi * ATT_WIDTH) // ATT_WIDTH, COL_K // ATT_WIDTH, COL_V // ATT_WIDTH
    cur = lambda c: (lambda b, r, n: (b, n, r * cw + c))
    prev = lambda c: (lambda b, r, n: (b, jnp.maximum(n - 1, 0), r * cw + c))
    o, lse = pl.pallas_call(
        _attn_kernel,
        grid=(BATCH, dil, nb),
        in_specs=[
            pl.BlockSpec((1, blk, ATT_WIDTH), cur(qc)),
            pl.BlockSpec((1, blk, ATT_WIDTH), prev(kc)),
            pl.BlockSpec((1, blk, ATT_WIDTH), cur(kc)),
            pl.BlockSpec((1, blk, ATT_WIDTH), prev(vc)),
            pl.BlockSpec((1, blk, ATT_WIDTH), cur(vc)),
            pl.BlockSpec((1, ATT_PAIRS, 2 * blk, 2 * blk), lambda b, r, n: (gi, 0, 0, 0)),
        ],
        out_specs=[
            pl.BlockSpec((1, blk, ATT_WIDTH), lambda b, r, n: (b, n, r)),
            pl.BlockSpec((1, blk, LANE), lambda b, r, n: (b, n, r)),
        ],
        out_shape=[
            jax.ShapeDtypeStruct((BATCH, length, dil * ATT_WIDTH), BF16),
            jax.ShapeDtypeStruct((BATCH, length, dil * LANE), F32),
        ],
        compiler_params=_params("parallel", "parallel", "arbitrary"),
        name=f"attn_d{dil}",
    )(zv, zv, zv, zv, zv, bias)
    return o.reshape(BATCH * SEQ, ATT_WIDTH), lse.reshape(BATCH * SEQ, LANE)


def _outproj_kernel(y_ref, o0_ref, o1_ref, o2_ref, l0_ref, l1_ref, l2_ref, e_ref, w_ref, x_ref, g_ref,
                    out_ref):
    l0, l1, l2 = l0_ref[...], l1_ref[...], l2_ref[...]
    mx = jnp.maximum(jnp.maximum(l0, l1), l2)
    w0, w1, w2 = jnp.exp(l0 - mx), jnp.exp(l1 - mx), jnp.exp(l2 - mx)
    inv = 1.0 / (w0 + w1 + w2)
    e = e_ref[...]
    att = _expand(w0 * inv, e, 2) * o0_ref[...].astype(F32)
    att = att + _expand(w1 * inv, e, 2) * o1_ref[...].astype(F32)
    att = att + _expand(w2 * inv, e, 2) * o2_ref[...].astype(F32)
    mix = _dot(y_ref[...], w_ref[0:SSM_D_INNER, :]) + _dot(att.astype(BF16), w_ref[SSM_D_INNER:, :])
    out_ref[...] = x_ref[...] + g_ref[0] * mix


def _outproj(y, outs, lses, w_out, x2, gate):
    m = x2.shape[0]
    tm = 512
    per_b = SEQ // tm
    e_np = np.zeros((LANE, ATT_WIDTH), np.float32)
    for h in range(ATT_KV_HEADS):
        e_np[h, h * ATT_HEAD_DIM:(h + 1) * ATT_HEAD_DIM] = 1.0
    e = jnp.asarray(e_np, BF16)
    row = lambda i: (i, 0)
    const = lambda i: (0, 0)
    return pl.pallas_call(
        _outproj_kernel,
        grid=(m // tm,),
        in_specs=[
            pl.BlockSpec((tm, SSM_D_INNER), row),
            pl.BlockSpec((tm, ATT_WIDTH), row),
            pl.BlockSpec((tm, ATT_WIDTH), row),
            pl.BlockSpec((tm, ATT_WIDTH), row),
            pl.BlockSpec((tm, LANE), row),
            pl.BlockSpec((tm, LANE), row),
            pl.BlockSpec((tm, LANE), row),
            pl.BlockSpec((LANE, ATT_WIDTH), const),
            pl.BlockSpec((SSM_D_INNER + ATT_WIDTH, D_MODEL), const),
            pl.BlockSpec((tm, D_MODEL), row),
            pl.BlockSpec((1, 1, D_MODEL), lambda i: (i // per_b, 0, 0)),
        ],
        out_specs=pl.BlockSpec((tm, D_MODEL), row),
        out_shape=jax.ShapeDtypeStruct((m, D_MODEL), F32),
        compiler_params=_params("parallel"),
        name="outproj",
    )(y, outs[0], outs[1], outs[2], lses[0], lses[1], lses[2], e, w_out, x2, gate)


def _ffn_kernel(final_norm, x_ref, g_ref, sc_ref, sh_ref, gate_ref, wg_ref, wu_ref, wd_ref, *rest):
    if final_norm:
        fg_ref, o_ref, h_ref, acc_ref = rest
    else:
        o_ref, h_ref, acc_ref = rest
    j = pl.program_id(1)

    @pl.when(j == 0)
    def _():
        h_ref[...] = _norm_mod(x_ref[...], g_ref[...], sc_ref[0], sh_ref[0]).astype(BF16)

    h = h_ref[...]
    a = (_silu(_dot(h, wg_ref[...])) * _dot(h, wu_ref[...])).astype(BF16)
    part = _dot(a, wd_ref[...])

    @pl.when(j == 0)
    def _():
        acc_ref[...] = part

    @pl.when(j > 0)
    def _():
        acc_ref[...] += part

    @pl.when(j == pl.num_programs(1) - 1)
    def _():
        y = x_ref[...] + gate_ref[0] * acc_ref[...]
        if final_norm:
            y = y * lax.rsqrt(jnp.mean(y * y, -1, keepdims=True) + EPS) * fg_ref[...]
        o_ref[...] = y


def _ffn(x2, g, sc, sh, gate, wg, wu, wd, final_g=None):
    m = x2.shape[0]
    tm, th = 1024, 256
    per_b = SEQ // tm
    final_norm = final_g is not None
    batch = lambda i, j: (i // per_b, 0, 0)
    in_specs = [
        pl.BlockSpec((tm, D_MODEL), lambda i, j: (i, 0)),
        pl.BlockSpec((1, D_MODEL), lambda i, j: (0, 0)),
        pl.BlockSpec((1, 1, D_MODEL), batch),
        pl.BlockSpec((1, 1, D_MODEL), batch),
        pl.BlockSpec((1, 1, D_MODEL), batch),
        pl.BlockSpec((D_MODEL, th), lambda i, j: (0, j)),
        pl.BlockSpec((D_MODEL, th), lambda i, j: (0, j)),
        pl.BlockSpec((th, D_MODEL), lambda i, j: (j, 0)),
    ]
    args = [x2, g, sc, sh, gate, wg, wu, wd]
    if final_norm:
        in_specs.append(pl.BlockSpec((1, D_MODEL), lambda i, j: (0, 0)))
        args.append(final_g)
    return pl.pallas_call(
        functools.partial(_ffn_kernel, final_norm),
        grid=(m // tm, FFN_HIDDEN // th),
        in_specs=in_specs,
        out_specs=pl.BlockSpec((tm, D_MODEL), lambda i, j: (i, 0)),
        out_shape=jax.ShapeDtypeStruct((m, D_MODEL), F32),
        scratch_shapes=[pltpu.VMEM((tm, D_MODEL), BF16), pltpu.VMEM((tm, D_MODEL), F32)],
        compiler_params=_params("parallel", "arbitrary"),
        name="ffn_final" if final_norm else "ffn",
    )(*args)


def _glu_kernel(x_ref, g_ref, sc_ref, sh_ref, wa_ref, wg_ref, ba_ref, bg_ref, o_ref, h_ref):
    @pl.when(pl.program_id(1) == 0)
    def _():
        h_ref[...] = _norm_mod(x_ref[...], g_ref[...], sc_ref[0], sh_ref[0]).astype(BF16)

    h = h_ref[...]
    a = _dot(h, wa_ref[...]) + ba_ref[...]
    gt = _dot(h, wg_ref[...]) + bg_ref[...]
    o_ref[...] = (a * _sigmoid(gt)).astype(o_ref.dtype)


def _glu(x2, g, sc, sh, w1, b1):
    m = x2.shape[0]
    tm, tn = 1024, 512
    per_b = SEQ // tm
    half = D_MODEL // tn
    batch = lambda i, j: (i // per_b, 0, 0)
    b1r = b1.reshape(1, 2 * D_MODEL)
    return pl.pallas_call(
        _glu_kernel,
        grid=(m // tm, half),
        in_specs=[
            pl.BlockSpec((tm, D_MODEL), lambda i, j: (i, 0)),
            pl.BlockSpec((1, D_MODEL), lambda i, j: (0, 0)),
            pl.BlockSpec((1, 1, D_MODEL), batch),
            pl.BlockSpec((1, 1, D_MODEL), batch),
            pl.BlockSpec((D_MODEL, tn), lambda i, j: (0, j)),
            pl.BlockSpec((D_MODEL, tn), lambda i, j: (0, j + half)),
            pl.BlockSpec((1, tn), lambda i, j: (0, j)),
            pl.BlockSpec((1, tn), lambda i, j: (0, j + half)),
        ],
        out_specs=pl.BlockSpec((tm, tn), lambda i, j: (i, j)),
        out_shape=jax.ShapeDtypeStruct((m, D_MODEL), F32),
        scratch_shapes=[pltpu.VMEM((tm, D_MODEL), BF16)],
        compiler_params=_params("parallel", "arbitrary"),
        name="glu",
    )(x2, g, sc, sh, w1, w1, b1r, b1r)


def _cconv_kernel(ts, rc, u_ref, halo_ref, wdw_ref, bdw_ref, lng_ref, lnb_ref, w2_ref, b2_ref, x_ref,
                  gate_ref, o_ref, buf, conv):
    @pl.when(pl.program_id(1) == 0)
    def _():
        buf[0:CONV_HALO, :] = jnp.zeros((CONV_HALO, D_MODEL), F32)

    @pl.when(pl.program_id(1) > 0)
    def _():
        buf[0:CONV_HALO, :] = halo_ref[0]

    buf[CONV_HALO:CONV_HALO + ts, :] = u_ref[0]
    off = CONV_HALO - (CONV_WIDTH - 1)
    bias = bdw_ref[...]

    def chunk(c, carry):
        r0 = pl.multiple_of(c * rc, rc)
        win = buf[pl.ds(r0, rc + CONV_HALO), :]
        acc = jnp.broadcast_to(bias, (rc, D_MODEL))
        for k in range(CONV_WIDTH):
            acc = acc + wdw_ref[k:k + 1, :] * win[off + k:off + k + rc, :]
        conv[pl.ds(r0, rc), :] = acc
        return carry

    lax.fori_loop(0, ts // rc, chunk, 0)
    u = conv[...]
    mu = jnp.mean(u, -1, keepdims=True)
    d = u - mu
    var = jnp.mean(d * d, -1, keepdims=True)
    v = _silu(d * lax.rsqrt(var + EPS) * lng_ref[...] + lnb_ref[...])
    mix = _dot(v.astype(BF16), w2_ref[...]) + b2_ref[...]
    o_ref[0] = x_ref[0] + gate_ref[0] * mix


def _cconv(u, w_dw, b_dw, ln_g, ln_b, w2, b2, x2, gate):
    ts, rc = 512, 16
    nt = SEQ // ts
    hb = ts // CONV_HALO
    u3 = u.reshape(BATCH, SEQ, D_MODEL)
    x3 = x2.reshape(BATCH, SEQ, D_MODEL)
    wdw = jnp.pad(w_dw, ((0, CONV_HALO - CONV_WIDTH), (0, 0)))
    vec = lambda v: v.reshape(1, D_MODEL)
    const = lambda b, i: (0, 0)
    out = pl.pallas_call(
        functools.partial(_cconv_kernel, ts, rc),
        grid=(BATCH, nt),
        in_specs=[
            pl.BlockSpec((1, ts, D_MODEL), lambda b, i: (b, i, 0)),
            pl.BlockSpec((1, CONV_HALO, D_MODEL), lambda b, i: (b, jnp.maximum(i * hb - 1, 0), 0)),
            pl.BlockSpec((CONV_HALO, D_MODEL), const),
            pl.BlockSpec((1, D_MODEL), const),
            pl.BlockSpec((1, D_MODEL), const),
            pl.BlockSpec((1, D_MODEL), const),
            pl.BlockSpec((D_MODEL, D_MODEL), const),
            pl.BlockSpec((1, D_MODEL), const),
            pl.BlockSpec((1, ts, D_MODEL), lambda b, i: (b, i, 0)),
            pl.BlockSpec((1, 1, D_MODEL), lambda b, i: (b, 0, 0)),
        ],
        out_specs=pl.BlockSpec((1, ts, D_MODEL), lambda b, i: (b, i, 0)),
        out_shape=jax.ShapeDtypeStruct((BATCH, SEQ, D_MODEL), F32),
        scratch_shapes=[pltpu.VMEM((CONV_HALO + ts, D_MODEL), F32), pltpu.VMEM((ts, D_MODEL), F32)],
        compiler_params=_params("parallel", "arbitrary"),
        name="cconv",
    )(u3, u3, wdw, vec(b_dw), vec(ln_g), vec(ln_b), w2, vec(b2), x3, gate)
    return out.reshape(BATCH * SEQ, D_MODEL)


def kernel(x, c, ada_w, ada_b, norm_mix_g, norm_ffn_g, hy_w_in, hy_conv_w, hy_conv_b, hy_dt_bias, hy_a_log,
           hy_d_skip, hy_ssm_norm_g, hy_w_out, rel_table, cv_w_pw1, cv_b_pw1, cv_w_dw, cv_b_dw, cv_ln_g,
           cv_ln_b, cv_w_pw2, cv_b_pw2, ffn_w_gate, ffn_w_up, ffn_w_down, final_norm_g):
    assert x.shape == (BATCH, SEQ, D_MODEL) and c.shape == (BATCH, D_MODEL)
    m = BATCH * SEQ
    x2 = x.reshape(m, D_MODEL)
    mod = _modulation(c, ada_w, ada_b)

    def mods(i):
        parts = jnp.split(mod[i, :BATCH], N_MOD, axis=-1)
        return [p.reshape(BATCH, 1, D_MODEL) for p in parts]

    vec = lambda v: v.reshape(1, D_MODEL)

    sh1, sc1, g1, sh2, sc2, g2 = mods(0)
    w_in = hy_w_in[0]
    s_z, s_xbc, s_dt = SSM_D_INNER, SSM_D_INNER + SSM_D_INNER + SSM_BC, SSM_D_INNER + SSM_D_INNER + SSM_BC + SSM_HEADS
    s_q = s_dt + ATT_GROUPS * ATT_WIDTH
    w_main = jnp.concatenate([
        w_in[:, :s_xbc],
        w_in[:, s_dt:s_q] * (ATT_HEAD_DIM ** -0.5),
        w_in[:, s_q:],
    ], axis=1).astype(BF16)
    w_dt = jnp.pad(w_in[:, s_xbc:s_dt], ((0, 0), (0, LANE - SSM_HEADS))).astype(BF16)
    zx, dt_raw = _inproj(x2, vec(norm_mix_g[0]), sc1, sh1, w_main, w_dt)
    y = _ssd(zx, dt_raw, hy_conv_w[0], hy_conv_b[0], hy_dt_bias[0], hy_a_log[0], hy_d_skip[0],
             hy_ssm_norm_g[0])
    bias = _rel_bias(rel_table)
    outs, lses = [], []
    for gi in range(ATT_GROUPS):
        o, lse = _attention(zx, bias, gi)
        outs.append(o)
        lses.append(lse)
    x2 = _outproj(y, outs, lses, hy_w_out[0].astype(BF16), x2, g1)
    x2 = _ffn(x2, vec(norm_ffn_g[0]), sc2, sh2, g2, ffn_w_gate[0].astype(BF16), ffn_w_up[0].astype(BF16),
              ffn_w_down[0].astype(BF16))

    sh1, sc1, g1, sh2, sc2, g2 = mods(1)
    u = _glu(x2, vec(norm_mix_g[1]), sc1, sh1, cv_w_pw1[0].astype(BF16), cv_b_pw1[0])
    x2 = _cconv(u, cv_w_dw[0], cv_b_dw[0], cv_ln_g[0], cv_ln_b[0], cv_w_pw2[0].astype(BF16), cv_b_pw2[0],
                x2, g1)
    x2 = _ffn(x2, vec(norm_ffn_g[1]), sc2, sh2, g2, ffn_w_gate[1].astype(BF16), ffn_w_up[1].astype(BF16),
              ffn_w_down[1].astype(BF16), final_g=vec(final_norm_g))
    return x2.reshape(BATCH, SEQ, D_MODEL)
```

```python
import functools
import math

import numpy as np
import jax
import jax.numpy as jnp
from jax import lax
from jax.experimental import pallas as pl
from jax.experimental.pallas import tpu as pltpu

F32 = jnp.float32
BF16 = jnp.bfloat16

D_MODEL = 1024
BATCH = 4
SEQ = 4096
DEPTH = 2
N_MOD = 6
EPS = 1e-6
SSM_D_INNER = 2048
SSM_HEAD_DIM = 64
SSM_HEADS = 32
SSM_GROUPS = 4
SSM_HEADS_PER_GROUP = SSM_HEADS // SSM_GROUPS
SSM_STATE = 128
SSM_CONV = 4
SSM_CHUNK = 128
SSM_BC = 2 * SSM_GROUPS * SSM_STATE
ATT_HEAD_DIM = 64
ATT_KV_HEADS = 16
ATT_PATTERNS = ((128, 1), (512, 4), (2048, 16))
ATT_GROUPS = len(ATT_PATTERNS)
ATT_WIDTH = ATT_KV_HEADS * ATT_HEAD_DIM
ATT_BLK = 128
ATT_PAIRS = ATT_KV_HEADS // 2
REL_BUCKETS = 32
REL_MAX_DIST = 2048
CONV_WIDTH = 31
CONV_HALO = 32
FFN_HIDDEN = 2816

COL_Z = 0
COL_X = SSM_D_INNER
COL_BC = COL_X + SSM_D_INNER
COL_Q = COL_BC + SSM_BC
COL_K = COL_Q + ATT_GROUPS * ATT_WIDTH
COL_V = COL_K + ATT_WIDTH
IN_MAIN = COL_V + ATT_WIDTH

LANE = 128
NEG = -1e30
VMEM_LIMIT = 56 * 1024 * 1024


def _params(*sem):
    return pltpu.CompilerParams(dimension_semantics=sem, vmem_limit_bytes=VMEM_LIMIT)


def _sigmoid(x):
    return 1.0 / (1.0 + jnp.exp(-x))


def _silu(x):
    return x * _sigmoid(x)


def _norm_mod(x, g, sc, sh):
    y = x * lax.rsqrt(jnp.mean(x * x, -1, keepdims=True) + EPS)
    return (y * g) * (1.0 + sc) + sh


def _split3(v):
    hi = v.astype(BF16)
    r1 = v - hi.astype(F32)
    mid = r1.astype(BF16)
    lo = (r1 - mid.astype(F32)).astype(BF16)
    return hi, mid, lo


def _dot(a, b):
    return jnp.dot(a, b, preferred_element_type=F32)


def _dot_nt(a, b):
    return lax.dot_general(a, b, (((1,), (1,)), ((), ())), preferred_element_type=F32)


def _dot_tn(a, b):
    return lax.dot_general(a, b, (((0,), (0,)), ((), ())), preferred_element_type=F32)


def _expand(v, e, terms):
    parts = _split3(v)[:terms]
    out = _dot(parts[0], e)
    for p in parts[1:]:
        out = out + _dot(p, e)
    return out


def _mod_kernel(c_ref, w_ref, b_ref, o_ref):
    cs = _silu(c_ref[...])
    hi, mid, lo = _split3(cs)
    w = w_ref[0]
    whi = w.astype(BF16)
    wlo = (w - whi.astype(F32)).astype(BF16)
    acc = _dot(hi, whi) + _dot(mid, whi) + _dot(hi, wlo) + _dot(lo, whi) + _dot(mid, wlo)
    o_ref[0] = acc + b_ref[0]


def _modulation(c, ada_w, ada_b):
    tn = 1024
    n = N_MOD * D_MODEL
    c8 = jnp.concatenate([c, jnp.zeros((8 - BATCH, D_MODEL), F32)], axis=0)
    return pl.pallas_call(
        _mod_kernel,
        grid=(DEPTH, n // tn),
        in_specs=[
            pl.BlockSpec((8, D_MODEL), lambda i, j: (0, 0)),
            pl.BlockSpec((1, D_MODEL, tn), lambda i, j: (i, 0, j)),
            pl.BlockSpec((1, 1, tn), lambda i, j: (i, 0, j)),
        ],
        out_specs=pl.BlockSpec((1, 8, tn), lambda i, j: (i, 0, j)),
        out_shape=jax.ShapeDtypeStruct((DEPTH, 8, n), F32),
        compiler_params=_params("parallel", "parallel"),
        name="adaln_mod",
    )(c8, ada_w, ada_b.reshape(DEPTH, 1, n))


IN_TM = 1024
IN_TN = 1024
IN_J_Q = COL_Q // IN_TN
IN_J_K = IN_J_Q + ATT_GROUPS


def _inproj_kernel(x_ref, g_ref, sc_ref, sh_ref, w_ref, wdt_ref, zx_ref, q0_ref, q1_ref, q2_ref, dt_ref,
                   h_ref, acc_ref):
    j = pl.program_id(1)

    @pl.when(j == 0)
    def _():
        h = _norm_mod(x_ref[...], g_ref[...], sc_ref[0], sh_ref[0]).astype(BF16)
        h_ref[...] = h
        dt_ref[...] = _dot(h, wdt_ref[...])

    res = _dot(h_ref[...], w_ref[...])

    @pl.when(j < IN_J_Q)
    def _():
        zx_ref[...] = res.astype(BF16)

    @pl.when((j == IN_J_Q) | (j >= IN_J_K))
    def _():
        q0_ref[0, 0] = res.astype(BF16)

    @pl.when(j > IN_J_Q)
    def _():
        for c in range(IN_TN // LANE):
            acc_ref[c] = res[:, c * LANE:(c + 1) * LANE]

    def scatter(dst_ref, dil):
        rows = IN_TM // dil
        for r in range(dil):
            for c in range(IN_TN // LANE):
                dst_ref[0, r, :, c * LANE:(c + 1) * LANE] = (
                    acc_ref[c, pl.ds(r, rows, stride=dil), :].astype(BF16))

    @pl.when((j == IN_J_Q + 1) | (j >= IN_J_K))
    def _():
        scatter(q1_ref, ATT_PATTERNS[1][1])

    @pl.when(j >= IN_J_Q + 2)
    def _():
        scatter(q2_ref, ATT_PATTERNS[2][1])


def _inproj(x2, g, sc, sh, w_main, w_dt):
    m = x2.shape[0]
    tm, tn = IN_TM, IN_TN
    per_b = SEQ // tm
    qcol = lambda j: jnp.maximum(j - (IN_J_K - 1), 0)

    def qspec(dil):
        return pl.BlockSpec((1, dil, tm // dil, tn), lambda i, j: (i // per_b, 0, i % per_b, qcol(j)))

    def qshape(dil):
        return jax.ShapeDtypeStruct((BATCH, dil, SEQ // dil, 3 * ATT_WIDTH), BF16)

    dils = [d for _, d in ATT_PATTERNS]
    return pl.pallas_call(
        _inproj_kernel,
        grid=(m // tm, IN_MAIN // tn),
        in_specs=[
            pl.BlockSpec((tm, D_MODEL), lambda i, j: (i, 0)),
            pl.BlockSpec((1, D_MODEL), lambda i, j: (0, 0)),
            pl.BlockSpec((1, 1, D_MODEL), lambda i, j: (i // per_b, 0, 0)),
            pl.BlockSpec((1, 1, D_MODEL), lambda i, j: (i // per_b, 0, 0)),
            pl.BlockSpec((D_MODEL, tn), lambda i, j: (0, j)),
            pl.BlockSpec((D_MODEL, LANE), lambda i, j: (0, 0)),
        ],
        out_specs=[
            pl.BlockSpec((tm, tn), lambda i, j: (i, jnp.minimum(j, IN_J_Q - 1))),
            qspec(dils[0]), qspec(dils[1]), qspec(dils[2]),
            pl.BlockSpec((tm, LANE), lambda i, j: (i, 0)),
        ],
        out_shape=[
            jax.ShapeDtypeStruct((m, COL_Q), BF16),
            qshape(dils[0]), qshape(dils[1]), qshape(dils[2]),
            jax.ShapeDtypeStruct((m, LANE), F32),
        ],
        scratch_shapes=[pltpu.VMEM((tm, D_MODEL), BF16), pltpu.VMEM((tn // LANE, tm, LANE), F32)],
        compiler_params=_params("parallel", "arbitrary"),
        name="inproj",
    )(x2, g, sc, sh, w_main, w_dt)


def _ssd_kernel(z_ref, xs_ref, bc_ref, dt_ref, cw_ref, cb_ref, dtb_ref, alog_ref, dskip_ref, ng_ref,
                tril_ref, e_ref, o_ref, xbuf, state, yacc):
    q = SSM_CHUNK
    width = SSM_D_INNER + SSM_BC

    @pl.when(pl.program_id(1) == 0)
    def _():
        xbuf[0:8, :] = jnp.zeros((8, width), F32)
        state[...] = jnp.zeros(state.shape, F32)

    xbuf[8:8 + q, 0:SSM_D_INNER] = xs_ref[...].astype(F32)
    xbuf[8:8 + q, SSM_D_INNER:width] = bc_ref[...].astype(F32)
    acc = cb_ref[...] + cw_ref[0:1, :] * xbuf[pl.ds(8 - (SSM_CONV - 1), q), :]
    for k in range(1, SSM_CONV):
        acc = acc + cw_ref[k:k + 1, :] * xbuf[pl.ds(8 - (SSM_CONV - 1) + k, q), :]
    xbuf[0:8, :] = xbuf[q:q + 8, :]
    xc = _silu(acc)
    xs = xc[:, 0:SSM_D_INNER]

    dtr = dt_ref[...] + dtb_ref[...]
    dt = jnp.maximum(dtr, 0.0) + jnp.log(1.0 + jnp.exp(-jnp.abs(dtr)))
    a = dt * (-jnp.exp(alog_ref[...]))
    tril = tril_ref[...]
    a_hi, a_mid, a_lo = _split3(a)
    a_cs = _dot(tril, a_hi) + _dot(tril, a_mid) + _dot(tril, a_lo)
    a_cs_t = a_cs.T
    e = e_ref[...]
    a_cs_e = _expand(a_cs, e, 3)
    dt_e = _expand(dt, e, 2)
    last_e = a_cs_e[q - 1:q, :]
    xdt = xs * dt_e
    xdt_b = xdt.astype(BF16)
    xdtd_b = (xdt * jnp.exp(last_e - a_cs_e)).astype(BF16)
    exp_acs_e = jnp.exp(a_cs_e)
    chunk_dec_e = jnp.exp(last_e)

    row_i = lax.broadcasted_iota(jnp.int32, (q, q), 0)
    col_i = lax.broadcasted_iota(jnp.int32, (q, q), 1)
    causal = row_i >= col_i
    first_half = col_i < SSM_HEAD_DIM

    gw = SSM_HEADS_PER_GROUP * SSM_HEAD_DIM
    for g in range(SSM_GROUPS):
        bg = xc[:, SSM_D_INNER + g * SSM_STATE:SSM_D_INNER + (g + 1) * SSM_STATE].astype(BF16)
        c0 = SSM_D_INNER + SSM_GROUPS * SSM_STATE + g * SSM_STATE
        cg = xc[:, c0:c0 + SSM_STATE].astype(BF16)
        cb = _dot_nt(cg, bg)
        sg = state[:, g * gw:(g + 1) * gw]
        y_off = _dot(cg, sg.astype(BF16)) * exp_acs_e[:, g * gw:(g + 1) * gw]
        for pr in range(SSM_HEADS_PER_GROUP // 2):
            lo_col = g * gw + pr * LANE
            rhs = xdt_b[:, lo_col:lo_col + LANE]
            halves = []
            for hh in range(2):
                h = g * SSM_HEADS_PER_GROUP + 2 * pr + hh
                seg = a_cs[:, h:h + 1] - a_cs_t[h:h + 1, :]
                dec = jnp.exp(jnp.where(causal, seg, NEG))
                halves.append(_dot((cb * dec).astype(BF16), rhs))
            yd = jnp.where(first_half, halves[0], halves[1])
            yacc[:, lo_col:lo_col + LANE] = yd + y_off[:, pr * LANE:(pr + 1) * LANE]
        s_new = _dot_tn(bg, xdtd_b[:, g * gw:(g + 1) * gw])
        state[:, g * gw:(g + 1) * gw] = chunk_dec_e[:, g * gw:(g + 1) * gw] * sg + s_new

    y = yacc[...] + dskip_ref[...] * xs
    u = y * _silu(z_ref[...].astype(F32))
    o = u * lax.rsqrt(jnp.mean(u * u, -1, keepdims=True) + EPS) * ng_ref[...]
    o_ref[...] = o.astype(o_ref.dtype)


def _ssd(zx, dt_raw, conv_w, conv_b, dt_bias, a_log, d_skip, norm_g):
    m = zx.shape[0]
    q = SSM_CHUNK
    nc = SEQ // q
    width = SSM_D_INNER + SSM_BC
    pad = LANE - SSM_HEADS
    dtb = jnp.pad(dt_bias, (0, pad)).reshape(1, LANE)
    alog = jnp.pad(a_log, (0, pad)).reshape(1, LANE)
    dskip_e = jnp.repeat(d_skip, SSM_HEAD_DIM).reshape(1, SSM_D_INNER)
    tril = jnp.asarray(np.tril(np.ones((q, q), np.float32)), BF16)
    e_np = np.zeros((LANE, SSM_D_INNER), np.float32)
    for h in range(SSM_HEADS):
        e_np[h, h * SSM_HEAD_DIM:(h + 1) * SSM_HEAD_DIM] = 1.0
    e = jnp.asarray(e_np, BF16)
    row = lambda b, c: (b * nc + c, 0)
    const = lambda b, c: (0, 0)
    return pl.pallas_call(
        _ssd_kernel,
        grid=(BATCH, nc),
        in_specs=[
            pl.BlockSpec((q, SSM_D_INNER), lambda b, c: (b * nc + c, COL_Z // SSM_D_INNER)),
            pl.BlockSpec((q, SSM_D_INNER), lambda b, c: (b * nc + c, COL_X // SSM_D_INNER)),
            pl.BlockSpec((q, SSM_BC), lambda b, c: (b * nc + c, COL_BC // SSM_BC)),
            pl.BlockSpec((q, LANE), row),
            pl.BlockSpec((SSM_CONV, width), const),
            pl.BlockSpec((1, width), const),
            pl.BlockSpec((1, LANE), const),
            pl.BlockSpec((1, LANE), const),
            pl.BlockSpec((1, SSM_D_INNER), const),
            pl.BlockSpec((1, SSM_D_INNER), const),
            pl.BlockSpec((q, q), const),
            pl.BlockSpec((LANE, SSM_D_INNER), const),
        ],
        out_specs=pl.BlockSpec((q, SSM_D_INNER), row),
        out_shape=jax.ShapeDtypeStruct((m, SSM_D_INNER), BF16),
        scratch_shapes=[
            pltpu.VMEM((q + 8, width), F32),
            pltpu.VMEM((SSM_STATE, SSM_D_INNER), F32),
            pltpu.VMEM((q, SSM_D_INNER), F32),
        ],
        compiler_params=_params("parallel", "arbitrary"),
        name="ssd",
    )(zx, zx, zx, dt_raw, conv_w, conv_b.reshape(1, width), dtb, alog, dskip_e,
      norm_g.reshape(1, SSM_D_INNER), tril, e)


def _t5_bucket_np(dist):
    max_exact = REL_BUCKETS // 2
    n = np.maximum(dist, 1).astype(np.float32)
    large = max_exact + (np.log(n / np.float32(max_exact)) / np.float32(math.log(REL_MAX_DIST / max_exact))
                         * np.float32(REL_BUCKETS - max_exact))
    large = np.minimum(large.astype(np.int32), REL_BUCKETS - 1)
    return np.where(dist < max_exact, dist, large).astype(np.int32)


def _bucket_table():
    i = np.arange(ATT_BLK)[:, None]
    j = np.arange(2 * ATT_BLK)[None, :]
    delta = ATT_BLK + i - j
    band = (delta >= 0) & (delta <= ATT_BLK)
    out = []
    for _, dil in ATT_PATTERNS:
        out.append(np.where(band, _t5_bucket_np(np.maximum(delta, 0) * dil), -1))
    return np.stack(out).astype(np.int32)


def _bias_kernel(tab_ref, bucket_ref, o_ref):
    g = pl.program_id(0)
    p = pl.program_id(1)
    bucket = bucket_ref[0]
    for hh in range(2):
        head = g * ATT_KV_HEADS + 2 * p + hh
        acc = jnp.full((ATT_BLK, 2 * ATT_BLK), NEG, F32)
        for b in range(REL_BUCKETS):
            acc = jnp.where(bucket == b, tab_ref[b, head], acc)
        o_ref[0, 0, hh * ATT_BLK:(hh + 1) * ATT_BLK, :] = acc


def _rel_bias(rel_table):
    buckets = jnp.asarray(_bucket_table())
    return pl.pallas_call(
        _bias_kernel,
        grid=(ATT_GROUPS, ATT_PAIRS),
        in_specs=[
            pl.BlockSpec(memory_space=pltpu.SMEM),
            pl.BlockSpec((1, ATT_BLK, 2 * ATT_BLK), lambda g, p: (g, 0, 0)),
        ],
        out_specs=pl.BlockSpec((1, 1, 2 * ATT_BLK, 2 * ATT_BLK), lambda g, p: (g, p, 0, 0)),
        out_shape=jax.ShapeDtypeStruct((ATT_GROUPS, ATT_PAIRS, 2 * ATT_BLK, 2 * ATT_BLK), F32),
        compiler_params=_params("parallel", "parallel"),
        name="rel_bias",
    )(rel_table, buckets)


def _attn_kernel(q_ref, kp_ref, kc_ref, vp_ref, vc_ref, bias_ref, o_ref, lse_ref):
    blk = ATT_BLK
    n = pl.program_id(2)
    col = lax.broadcasted_iota(jnp.int32, (1, 2 * blk), 1)
    pen = jnp.where(col < blk, NEG, 0.0) * (n == 0).astype(F32)
    lane = lax.broadcasted_iota(jnp.int32, (blk, LANE), 1)
    first = lane < ATT_HEAD_DIM
    lse_acc = jnp.zeros((blk, LANE), F32)
    for p in range(ATT_PAIRS):
        sl = slice(p * LANE, (p + 1) * LANE)
        q = q_ref[0, 0, :, sl]
        zero = jnp.zeros_like(q)
        qq = jnp.concatenate([jnp.where(first, q, zero), jnp.where(first, zero, q)], axis=0)
        kk = jnp.concatenate([kp_ref[0, 0, :, sl], kc_ref[0, 0, :, sl]], axis=0)
        vv = jnp.concatenate([vp_ref[0, 0, :, sl], vc_ref[0, 0, :, sl]], axis=0)
        s = _dot_nt(qq, kk) + bias_ref[0, p] + pen
        m = jnp.max(s, -1, keepdims=True)
        ex = jnp.exp(s - m)
        l = jnp.sum(ex, -1, keepdims=True)
        o2 = _dot(ex.astype(BF16), vv) / l
        o_ref[0, 0, :, sl] = jnp.where(first, o2[0:blk], o2[blk:2 * blk]).astype(o_ref.dtype)
        lse = m + jnp.log(l)
        lse_acc = jnp.where(lane == 2 * p, lse[0:blk], lse_acc)
        lse_acc = jnp.where(lane == 2 * p + 1, lse[blk:2 * blk], lse_acc)
    lse_ref[0, 0] = lse_acc


def _attention(qkv, bias, gi):
    dil = ATT_PATTERNS[gi][1]
    blk = ATT_BLK
    length = SEQ // dil
    nb = length // blk
    cur = lambda c: (lambda b, r, n: (b, r, n, c))
    prev = lambda c: (lambda b, r, n: (b, r, jnp.maximum(n - 1, 0), c))
    spec = lambda index_map: pl.BlockSpec((1, 1, blk, ATT_WIDTH), index_map)
    return pl.pallas_call(
        _attn_kernel,
        grid=(BATCH, dil, nb),
        in_specs=[
            spec(cur(0)), spec(prev(1)), spec(cur(1)), spec(prev(2)), spec(cur(2)),
            pl.BlockSpec((1, ATT_PAIRS, 2 * blk, 2 * blk), lambda b, r, n: (gi, 0, 0, 0)),
        ],
        out_specs=[
            spec(cur(0)),
            pl.BlockSpec((1, 1, blk, LANE), cur(0)),
        ],
        out_shape=[
            jax.ShapeDtypeStruct((BATCH, dil, length, ATT_WIDTH), BF16),
            jax.ShapeDtypeStruct((BATCH, dil, length, LANE), F32),
        ],
        compiler_params=_params("parallel", "parallel", "arbitrary"),
        name=f"attn_d{dil}",
    )(qkv, qkv, qkv, qkv, qkv, bias)


OUT_TM = 512


def _outproj_kernel(y_ref, o0_ref, o1_ref, o2_ref, l0_ref, l1_ref, l2_ref, e_ref, w_ref, x_ref, g_ref,
                    out_ref, os1, os2, ls1, ls2):
    def to_natural(src_ref, dst_ref, dil):
        rows = OUT_TM // dil
        for r in range(dil):
            for c in range(dst_ref.shape[0]):
                dst_ref[c, pl.ds(r, rows, stride=dil), :] = src_ref[0, r, :, c * LANE:(c + 1) * LANE].astype(F32)

    def natural(ref):
        return jnp.concatenate([ref[c] for c in range(ref.shape[0])], axis=1)

    to_natural(o1_ref, os1, ATT_PATTERNS[1][1])
    to_natural(o2_ref, os2, ATT_PATTERNS[2][1])
    to_natural(l1_ref, ls1, ATT_PATTERNS[1][1])
    to_natural(l2_ref, ls2, ATT_PATTERNS[2][1])
    l0, l1, l2 = l0_ref[0, 0], ls1[0], ls2[0]
    mx = jnp.maximum(jnp.maximum(l0, l1), l2)
    w0, w1, w2 = jnp.exp(l0 - mx), jnp.exp(l1 - mx), jnp.exp(l2 - mx)
    inv = 1.0 / (w0 + w1 + w2)
    e = e_ref[...]
    att = _expand(w0 * inv, e, 2) * o0_ref[0, 0].astype(F32)
    att = att + _expand(w1 * inv, e, 2) * natural(os1)
    att = att + _expand(w2 * inv, e, 2) * natural(os2)
    mix = _dot(y_ref[...], w_ref[0:SSM_D_INNER, :]) + _dot(att.astype(BF16), w_ref[SSM_D_INNER:, :])
    out_ref[...] = x_ref[...] + g_ref[0] * mix


def _outproj(y, outs, lses, w_out, x2, gate):
    m = x2.shape[0]
    tm = OUT_TM
    per_b = SEQ // tm
    e_np = np.zeros((LANE, ATT_WIDTH), np.float32)
    for h in range(ATT_KV_HEADS):
        e_np[h, h * ATT_HEAD_DIM:(h + 1) * ATT_HEAD_DIM] = 1.0
    e = jnp.asarray(e_np, BF16)
    row = lambda i: (i, 0)
    const = lambda i: (0, 0)
    dils = [d for _, d in ATT_PATTERNS]
    res = lambda dil, width: pl.BlockSpec((1, dil, tm // dil, width), lambda i: (i // per_b, 0, i % per_b, 0))
    return pl.pallas_call(
        _outproj_kernel,
        grid=(m // tm,),
        in_specs=[
            pl.BlockSpec((tm, SSM_D_INNER), row),
            res(dils[0], ATT_WIDTH), res(dils[1], ATT_WIDTH), res(dils[2], ATT_WIDTH),
            res(dils[0], LANE), res(dils[1], LANE), res(dils[2], LANE),
            pl.BlockSpec((LANE, ATT_WIDTH), const),
            pl.BlockSpec((SSM_D_INNER + ATT_WIDTH, D_MODEL), const),
            pl.BlockSpec((tm, D_MODEL), row),
            pl.BlockSpec((1, 1, D_MODEL), lambda i: (i // per_b, 0, 0)),
        ],
        out_specs=pl.BlockSpec((tm, D_MODEL), row),
        out_shape=jax.ShapeDtypeStruct((m, D_MODEL), F32),
        scratch_shapes=[
            pltpu.VMEM((ATT_WIDTH // LANE, tm, LANE), F32), pltpu.VMEM((ATT_WIDTH // LANE, tm, LANE), F32),
            pltpu.VMEM((1, tm, LANE), F32), pltpu.VMEM((1, tm, LANE), F32),
        ],
        compiler_params=_params("parallel"),
        name="outproj",
    )(y, outs[0], outs[1], outs[2], lses[0], lses[1], lses[2], e, w_out, x2, gate)


def _ffn_kernel(final_norm, x_ref, g_ref, sc_ref, sh_ref, gate_ref, wg_ref, wu_ref, wd_ref, *rest):
    if final_norm:
        fg_ref, o_ref, h_ref, acc_ref = rest
    else:
        o_ref, h_ref, acc_ref = rest
    j = pl.program_id(1)

    @pl.when(j == 0)
    def _():
        h_ref[...] = _norm_mod(x_ref[...], g_ref[...], sc_ref[0], sh_ref[0]).astype(BF16)

    h = h_ref[...]
    a = (_silu(_dot(h, wg_ref[...])) * _dot(h, wu_ref[...])).astype(BF16)
    part = _dot(a, wd_ref[...])

    @pl.when(j == 0)
    def _():
        acc_ref[...] = part

    @pl.when(j > 0)
    def _():
        acc_ref[...] += part

    @pl.when(j == pl.num_programs(1) - 1)
    def _():
        y = x_ref[...] + gate_ref[0] * acc_ref[...]
        if final_norm:
            y = y * lax.rsqrt(jnp.mean(y * y, -1, keepdims=True) + EPS) * fg_ref[...]
        o_ref[...] = y


def _ffn(x2, g, sc, sh, gate, wg, wu, wd, final_g=None):
    m = x2.shape[0]
    tm, th = 1024, 256
    per_b = SEQ // tm
    final_norm = final_g is not None
    batch = lambda i, j: (i // per_b, 0, 0)
    in_specs = [
        pl.BlockSpec((tm, D_MODEL), lambda i, j: (i, 0)),
        pl.BlockSpec((1, D_MODEL), lambda i, j: (0, 0)),
        pl.BlockSpec((1, 1, D_MODEL), batch),
        pl.BlockSpec((1, 1, D_MODEL), batch),
        pl.BlockSpec((1, 1, D_MODEL), batch),
        pl.BlockSpec((D_MODEL, th), lambda i, j: (0, j)),
        pl.BlockSpec((D_MODEL, th), lambda i, j: (0, j)),
        pl.BlockSpec((th, D_MODEL), lambda i, j: (j, 0)),
    ]
    args = [x2, g, sc, sh, gate, wg, wu, wd]
    if final_norm:
        in_specs.append(pl.BlockSpec((1, D_MODEL), lambda i, j: (0, 0)))
        args.append(final_g)
    return pl.pallas_call(
        functools.partial(_ffn_kernel, final_norm),
        grid=(m // tm, FFN_HIDDEN // th),
        in_specs=in_specs,
        out_specs=pl.BlockSpec((tm, D_MODEL), lambda i, j: (i, 0)),
        out_shape=jax.ShapeDtypeStruct((m, D_MODEL), F32),
        scratch_shapes=[pltpu.VMEM((tm, D_MODEL), BF16), pltpu.VMEM((tm, D_MODEL), F32)],
        compiler_params=_params("parallel", "arbitrary"),
        name="ffn_final" if final_norm else "ffn",
    )(*args)


def _glu_kernel(x_ref, g_ref, sc_ref, sh_ref, wa_ref, wg_ref, ba_ref, bg_ref, o_ref, h_ref):
    @pl.when(pl.program_id(1) == 0)
    def _():
        h_ref[...] = _norm_mod(x_ref[...], g_ref[...], sc_ref[0], sh_ref[0]).astype(BF16)

    h = h_ref[...]
    a = _dot(h, wa_ref[...]) + ba_ref[...]
    gt = _dot(h, wg_ref[...]) + bg_ref[...]
    o_ref[...] = (a * _sigmoid(gt)).astype(o_ref.dtype)


def _glu(x2, g, sc, sh, w1, b1):
    m = x2.shape[0]
    tm, tn = 1024, 512
    per_b = SEQ // tm
    half = D_MODEL // tn
    batch = lambda i, j: (i // per_b, 0, 0)
    b1r = b1.reshape(1, 2 * D_MODEL)
    return pl.pallas_call(
        _glu_kernel,
        grid=(m // tm, half),
        in_specs=[
            pl.BlockSpec((tm, D_MODEL), lambda i, j: (i, 0)),
            pl.BlockSpec((1, D_MODEL), lambda i, j: (0, 0)),
            pl.BlockSpec((1, 1, D_MODEL), batch),
            pl.BlockSpec((1, 1, D_MODEL), batch),
            pl.BlockSpec((D_MODEL, tn), lambda i, j: (0, j)),
            pl.BlockSpec((D_MODEL, tn), lambda i, j: (0, j + half)),
            pl.BlockSpec((1, tn), lambda i, j: (0, j)),
            pl.BlockSpec((1, tn), lambda i, j: (0, j + half)),
        ],
        out_specs=pl.BlockSpec((tm, tn), lambda i, j: (i, j)),
        out_shape=jax.ShapeDtypeStruct((m, D_MODEL), F32),
        scratch_shapes=[pltpu.VMEM((tm, D_MODEL), BF16)],
        compiler_params=_params("parallel", "arbitrary"),
        name="glu",
    )(x2, g, sc, sh, w1, w1, b1r, b1r)


def _cconv_kernel(ts, rc, u_ref, halo_ref, wdw_ref, bdw_ref, lng_ref, lnb_ref, w2_ref, b2_ref, x_ref,
                  gate_ref, o_ref, buf, conv):
    @pl.when(pl.program_id(1) == 0)
    def _():
        buf[0:CONV_HALO, :] = jnp.zeros((CONV_HALO, D_MODEL), F32)

    @pl.when(pl.program_id(1) > 0)
    def _():
        buf[0:CONV_HALO, :] = halo_ref[0]

    buf[CONV_HALO:CONV_HALO + ts, :] = u_ref[0]
    off = CONV_HALO - (CONV_WIDTH - 1)
    bias = bdw_ref[...]

    def chunk(c, carry):
        r0 = pl.multiple_of(c * rc, rc)
        win = buf[pl.ds(r0, rc + CONV_HALO), :]
        acc = jnp.broadcast_to(bias, (rc, D_MODEL))
        for k in range(CONV_WIDTH):
            acc = acc + wdw_ref[k:k + 1, :] * win[off + k:off + k + rc, :]
        conv[pl.ds(r0, rc), :] = acc
        return carry

    lax.fori_loop(0, ts // rc, chunk, 0)
    u = conv[...]
    mu = jnp.mean(u, -1, keepdims=True)
    d = u - mu
    var = jnp.mean(d * d, -1, keepdims=True)
    v = _silu(d * lax.rsqrt(var + EPS) * lng_ref[...] + lnb_ref[...])
    mix = _dot(v.astype(BF16), w2_ref[...]) + b2_ref[...]
    o_ref[0] = x_ref[0] + gate_ref[0] * mix


def _cconv(u, w_dw, b_dw, ln_g, ln_b, w2, b2, x2, gate):
    ts, rc = 512, 16
    nt = SEQ // ts
    hb = ts // CONV_HALO
    u3 = u.reshape(BATCH, SEQ, D_MODEL)
    x3 = x2.reshape(BATCH, SEQ, D_MODEL)
    wdw = jnp.pad(w_dw, ((0, CONV_HALO - CONV_WIDTH), (0, 0)))
    vec = lambda v: v.reshape(1, D_MODEL)
    const = lambda b, i: (0, 0)
    out = pl.pallas_call(
        functools.partial(_cconv_kernel, ts, rc),
        grid=(BATCH, nt),
        in_specs=[
            pl.BlockSpec((1, ts, D_MODEL), lambda b, i: (b, i, 0)),
            pl.BlockSpec((1, CONV_HALO, D_MODEL), lambda b, i: (b, jnp.maximum(i * hb - 1, 0), 0)),
            pl.BlockSpec((CONV_HALO, D_MODEL), const),
            pl.BlockSpec((1, D_MODEL), const),
            pl.BlockSpec((1, D_MODEL), const),
            pl.BlockSpec((1, D_MODEL), const),
            pl.BlockSpec((D_MODEL, D_MODEL), const),
            pl.BlockSpec((1, D_MODEL), const),
            pl.BlockSpec((1, ts, D_MODEL), lambda b, i: (b, i, 0)),
            pl.BlockSpec((1, 1, D_MODEL), lambda b, i: (b, 0, 0)),
        ],
        out_specs=pl.BlockSpec((1, ts, D_MODEL), lambda b, i: (b, i, 0)),
        out_shape=jax.ShapeDtypeStruct((BATCH, SEQ, D_MODEL), F32),
        scratch_shapes=[pltpu.VMEM((CONV_HALO + ts, D_MODEL), F32), pltpu.VMEM((ts, D_MODEL), F32)],
        compiler_params=_params("parallel", "arbitrary"),
        name="cconv",
    )(u3, u3, wdw, vec(b_dw), vec(ln_g), vec(ln_b), w2, vec(b2), x3, gate)
    return out.reshape(BATCH * SEQ, D_MODEL)


def kernel(x, c, ada_w, ada_b, norm_mix_g, norm_ffn_g, hy_w_in, hy_conv_w, hy_conv_b, hy_dt_bias, hy_a_log,
           hy_d_skip, hy_ssm_norm_g, hy_w_out, rel_table, cv_w_pw1, cv_b_pw1, cv_w_dw, cv_b_dw, cv_ln_g,
           cv_ln_b, cv_w_pw2, cv_b_pw2, ffn_w_gate, ffn_w_up, ffn_w_down, final_norm_g):
    assert x.shape == (BATCH, SEQ, D_MODEL) and c.shape == (BATCH, D_MODEL)
    m = BATCH * SEQ
    x2 = x.reshape(m, D_MODEL)
    mod = _modulation(c, ada_w, ada_b)

    def mods(i):
        parts = jnp.split(mod[i, :BATCH], N_MOD, axis=-1)
        return [p.reshape(BATCH, 1, D_MODEL) for p in parts]

    vec = lambda v: v.reshape(1, D_MODEL)

    sh1, sc1, g1, sh2, sc2, g2 = mods(0)
    w_in = hy_w_in[0]
    s_z, s_xbc, s_dt = SSM_D_INNER, SSM_D_INNER + SSM_D_INNER + SSM_BC, SSM_D_INNER + SSM_D_INNER + SSM_BC + SSM_HEADS
    s_q = s_dt + ATT_GROUPS * ATT_WIDTH
    w_main = jnp.concatenate([
        w_in[:, :s_xbc],
        w_in[:, s_dt:s_q] * (ATT_HEAD_DIM ** -0.5),
        w_in[:, s_q:],
    ], axis=1).astype(BF16)
    w_dt = jnp.pad(w_in[:, s_xbc:s_dt], ((0, 0), (0, LANE - SSM_HEADS))).astype(BF16)
    zx, qkv0, qkv1, qkv2, dt_raw = _inproj(x2, vec(norm_mix_g[0]), sc1, sh1, w_main, w_dt)
    y = _ssd(zx, dt_raw, hy_conv_w[0], hy_conv_b[0], hy_dt_bias[0], hy_a_log[0], hy_d_skip[0],
             hy_ssm_norm_g[0])
    bias = _rel_bias(rel_table)
    outs, lses = [], []
    for gi, qkv in enumerate((qkv0, qkv1, qkv2)):
        o, lse = _attention(qkv, bias, gi)
        outs.append(o)
        lses.append(lse)
    x2 = _outproj(y, outs, lses, hy_w_out[0].astype(BF16), x2, g1)
    x2 = _ffn(x2, vec(norm_ffn_g[0]), sc2, sh2, g2, ffn_w_gate[0].astype(BF16), ffn_w_up[0].astype(BF16),
              ffn_w_down[0].astype(BF16))

    sh1, sc1, g1, sh2, sc2, g2 = mods(1)
    u = _glu(x2, vec(norm_mix_g[1]), sc1, sh1, cv_w_pw1[0].astype(BF16), cv_b_pw1[0])
    x2 = _cconv(u, cv_w_dw[0], cv_b_dw[0], cv_ln_g[0], cv_ln_b[0], cv_w_pw2[0].astype(BF16), cv_b_pw2[0],
                x2, g1)
    x2 = _ffn(x2, vec(norm_ffn_g[1]), sc2, sh2, g2, ffn_w_gate[1].astype(BF16), ffn_w_up[1].astype(BF16),
              ffn_w_down[1].astype(BF16), final_g=vec(final_norm_g))
    return x2.reshape(BATCH, SEQ, D_MODEL)
```

```python
import functools
import math

import numpy as np
import jax
import jax.numpy as jnp
from jax import lax
from jax.experimental import pallas as pl
from jax.experimental.pallas import tpu as pltpu

F32 = jnp.float32
BF16 = jnp.bfloat16

D_MODEL = 1024
BATCH = 4
SEQ = 4096
DEPTH = 2
N_MOD = 6
EPS = 1e-6
SSM_D_INNER = 2048
SSM_HEAD_DIM = 64
SSM_HEADS = 32
SSM_GROUPS = 4
SSM_HEADS_PER_GROUP = SSM_HEADS // SSM_GROUPS
SSM_STATE = 128
SSM_CONV = 4
SSM_CHUNK = 128
SSM_BC = 2 * SSM_GROUPS * SSM_STATE
ATT_HEAD_DIM = 64
ATT_KV_HEADS = 16
ATT_PATTERNS = ((128, 1), (512, 4), (2048, 16))
ATT_GROUPS = len(ATT_PATTERNS)
ATT_WIDTH = ATT_KV_HEADS * ATT_HEAD_DIM
ATT_BLK = 128
ATT_PAIRS = ATT_KV_HEADS // 2
REL_BUCKETS = 32
REL_MAX_DIST = 2048
CONV_WIDTH = 31
CONV_HALO = 32
FFN_HIDDEN = 2816

COL_Z = 0
COL_X = SSM_D_INNER
COL_BC = COL_X + SSM_D_INNER
COL_Q = COL_BC + SSM_BC
COL_K = COL_Q + ATT_GROUPS * ATT_WIDTH
COL_V = COL_K + ATT_WIDTH
IN_MAIN = COL_V + ATT_WIDTH

LANE = 128
NEG = -1e30
LOG2E = math.log2(math.e)
VMEM_LIMIT = 56 * 1024 * 1024


def _params(*sem):
    return pltpu.CompilerParams(dimension_semantics=sem, vmem_limit_bytes=VMEM_LIMIT)


def _sigmoid(x):
    return 1.0 / (1.0 + jnp.exp(-x))


def _silu(x):
    return x * _sigmoid(x)


def _norm_mod(x, g, sc, sh):
    y = x * lax.rsqrt(jnp.mean(x * x, -1, keepdims=True) + EPS)
    return (y * g) * (1.0 + sc) + sh


def _split3(v):
    hi = v.astype(BF16)
    r1 = v - hi.astype(F32)
    mid = r1.astype(BF16)
    lo = (r1 - mid.astype(F32)).astype(BF16)
    return hi, mid, lo


def _dot(a, b):
    return jnp.dot(a, b, preferred_element_type=F32)


def _dot_nt(a, b):
    return lax.dot_general(a, b, (((1,), (1,)), ((), ())), preferred_element_type=F32)


def _dot_tn(a, b):
    return lax.dot_general(a, b, (((0,), (0,)), ((), ())), preferred_element_type=F32)


def _expand(v, e, terms):
    parts = _split3(v)[:terms]
    out = _dot(parts[0], e)
    for p in parts[1:]:
        out = out + _dot(p, e)
    return out


def _mod_kernel(c_ref, w_ref, b_ref, o_ref):
    cs = _silu(c_ref[...])
    hi, mid, lo = _split3(cs)
    w = w_ref[0]
    whi = w.astype(BF16)
    wlo = (w - whi.astype(F32)).astype(BF16)
    acc = _dot(hi, whi) + _dot(mid, whi) + _dot(hi, wlo) + _dot(lo, whi) + _dot(mid, wlo)
    o_ref[0] = acc + b_ref[0]


def _modulation(c, ada_w, ada_b):
    tn = 1024
    n = N_MOD * D_MODEL
    c8 = jnp.concatenate([c, jnp.zeros((8 - BATCH, D_MODEL), F32)], axis=0)
    return pl.pallas_call(
        _mod_kernel,
        grid=(DEPTH, n // tn),
        in_specs=[
            pl.BlockSpec((8, D_MODEL), lambda i, j: (0, 0)),
            pl.BlockSpec((1, D_MODEL, tn), lambda i, j: (i, 0, j)),
            pl.BlockSpec((1, 1, tn), lambda i, j: (i, 0, j)),
        ],
        out_specs=pl.BlockSpec((1, 8, tn), lambda i, j: (i, 0, j)),
        out_shape=jax.ShapeDtypeStruct((DEPTH, 8, n), F32),
        compiler_params=_params("parallel", "parallel"),
        name="adaln_mod",
    )(c8, ada_w, ada_b.reshape(DEPTH, 1, n))


IN_TM = 1024
IN_TN = 1024
IN_J_Q = COL_Q // IN_TN
IN_J_K = IN_J_Q + ATT_GROUPS


def _inproj_kernel(x_ref, g_ref, sc_ref, sh_ref, w_ref, wdt_ref, zx_ref, q0_ref, q1_ref, q2_ref, dt_ref,
                   h_ref, acc_ref):
    j = pl.program_id(1)

    @pl.when(j == 0)
    def _():
        h = _norm_mod(x_ref[...], g_ref[...], sc_ref[0], sh_ref[0]).astype(BF16)
        h_ref[...] = h
        dt_ref[...] = _dot(h, wdt_ref[...])

    res = _dot(h_ref[...], w_ref[...])

    @pl.when(j < IN_J_Q)
    def _():
        zx_ref[...] = res.astype(BF16)

    @pl.when((j == IN_J_Q) | (j >= IN_J_K))
    def _():
        q0_ref[0, 0] = res.astype(BF16)

    @pl.when(j > IN_J_Q)
    def _():
        for c in range(IN_TN // LANE):
            acc_ref[c] = res[:, c * LANE:(c + 1) * LANE]

    def scatter(dst_ref, dil):
        rows = IN_TM // dil
        for r in range(dil):
            for c in range(IN_TN // LANE):
                dst_ref[0, r, :, c * LANE:(c + 1) * LANE] = (
                    acc_ref[c, pl.ds(r, rows, stride=dil), :].astype(BF16))

    @pl.when((j == IN_J_Q + 1) | (j >= IN_J_K))
    def _():
        scatter(q1_ref, ATT_PATTERNS[1][1])

    @pl.when(j >= IN_J_Q + 2)
    def _():
        scatter(q2_ref, ATT_PATTERNS[2][1])


def _inproj(x2, g, sc, sh, w_main, w_dt):
    m = x2.shape[0]
    tm, tn = IN_TM, IN_TN
    per_b = SEQ // tm
    qcol = lambda j: jnp.maximum(j - (IN_J_K - 1), 0)

    def qspec(dil):
        return pl.BlockSpec((1, dil, tm // dil, tn), lambda i, j: (i // per_b, 0, i % per_b, qcol(j)))

    def qshape(dil):
        return jax.ShapeDtypeStruct((BATCH, dil, SEQ // dil, 3 * ATT_WIDTH), BF16)

    dils = [d for _, d in ATT_PATTERNS]
    return pl.pallas_call(
        _inproj_kernel,
        grid=(m // tm, IN_MAIN // tn),
        in_specs=[
            pl.BlockSpec((tm, D_MODEL), lambda i, j: (i, 0)),
            pl.BlockSpec((1, D_MODEL), lambda i, j: (0, 0)),
            pl.BlockSpec((1, 1, D_MODEL), lambda i, j: (i // per_b, 0, 0)),
            pl.BlockSpec((1, 1, D_MODEL), lambda i, j: (i // per_b, 0, 0)),
            pl.BlockSpec((D_MODEL, tn), lambda i, j: (0, j)),
            pl.BlockSpec((D_MODEL, LANE), lambda i, j: (0, 0)),
        ],
        out_specs=[
            pl.BlockSpec((tm, tn), lambda i, j: (i, jnp.minimum(j, IN_J_Q - 1))),
            qspec(dils[0]), qspec(dils[1]), qspec(dils[2]),
            pl.BlockSpec((tm, LANE), lambda i, j: (i, 0)),
        ],
        out_shape=[
            jax.ShapeDtypeStruct((m, COL_Q), BF16),
            qshape(dils[0]), qshape(dils[1]), qshape(dils[2]),
            jax.ShapeDtypeStruct((m, LANE), F32),
        ],
        scratch_shapes=[pltpu.VMEM((tm, D_MODEL), BF16), pltpu.VMEM((tn // LANE, tm, LANE), F32)],
        compiler_params=_params("parallel", "arbitrary"),
        name="inproj",
    )(x2, g, sc, sh, w_main, w_dt)


def _ssd_kernel(z_ref, xs_ref, bc_ref, dt_ref, cw_ref, cb_ref, dtb_ref, alog_ref, dskip_ref, ng_ref,
                tril_ref, e_ref, o_ref, xbuf, state, yacc):
    q = SSM_CHUNK
    width = SSM_D_INNER + SSM_BC

    @pl.when(pl.program_id(1) == 0)
    def _():
        xbuf[0:8, :] = jnp.zeros((8, width), F32)
        state[...] = jnp.zeros(state.shape, F32)

    xbuf[8:8 + q, 0:SSM_D_INNER] = xs_ref[...].astype(F32)
    xbuf[8:8 + q, SSM_D_INNER:width] = bc_ref[...].astype(F32)
    acc = cb_ref[...] + cw_ref[0:1, :] * xbuf[pl.ds(8 - (SSM_CONV - 1), q), :]
    for k in range(1, SSM_CONV):
        acc = acc + cw_ref[k:k + 1, :] * xbuf[pl.ds(8 - (SSM_CONV - 1) + k, q), :]
    xbuf[0:8, :] = xbuf[q:q + 8, :]
    xc = _silu(acc)
    xs = xc[:, 0:SSM_D_INNER]

    dtr = dt_ref[...] + dtb_ref[...]
    dt = jnp.maximum(dtr, 0.0) + jnp.log(1.0 + jnp.exp(-jnp.abs(dtr)))
    a = dt * (-LOG2E * jnp.exp(alog_ref[...]))
    tril = tril_ref[...]
    a_hi, a_mid, a_lo = _split3(a)
    a_cs = _dot(tril, a_hi) + _dot(tril, a_mid) + _dot(tril, a_lo)
    a_cs_t = a_cs.T
    e = e_ref[...]
    a_cs_e = _expand(a_cs, e, 3)
    dt_e = _expand(dt, e, 2)
    last_e = a_cs_e[q - 1:q, :]
    xdt = xs * dt_e
    xdt_b = xdt.astype(BF16)
    xdtd_b = (xdt * jnp.exp2(last_e - a_cs_e)).astype(BF16)
    exp_acs_e = jnp.exp2(a_cs_e)
    chunk_dec_e = jnp.exp2(last_e)

    row_i = lax.broadcasted_iota(jnp.int32, (q, q), 0)
    col_i = lax.broadcasted_iota(jnp.int32, (q, q), 1)
    causal = row_i >= col_i
    first_half = col_i < SSM_HEAD_DIM

    gw = SSM_HEADS_PER_GROUP * SSM_HEAD_DIM
    for g in range(SSM_GROUPS):
        bg = xc[:, SSM_D_INNER + g * SSM_STATE:SSM_D_INNER + (g + 1) * SSM_STATE].astype(BF16)
        c0 = SSM_D_INNER + SSM_GROUPS * SSM_STATE + g * SSM_STATE
        cg = xc[:, c0:c0 + SSM_STATE].astype(BF16)
        cb = _dot_nt(cg, bg)
        sg = state[:, g * gw:(g + 1) * gw]
        y_off = _dot(cg, sg.astype(BF16)) * exp_acs_e[:, g * gw:(g + 1) * gw]
        for pr in range(SSM_HEADS_PER_GROUP // 2):
            lo_col = g * gw + pr * LANE
            rhs = xdt_b[:, lo_col:lo_col + LANE]
            halves = []
            for hh in range(2):
                h = g * SSM_HEADS_PER_GROUP + 2 * pr + hh
                seg = a_cs[:, h:h + 1] - a_cs_t[h:h + 1, :]
                dec = jnp.exp2(jnp.where(causal, seg, NEG))
                halves.append(_dot((cb * dec).astype(BF16), rhs))
            yd = jnp.where(first_half, halves[0], halves[1])
            yacc[:, lo_col:lo_col + LANE] = yd + y_off[:, pr * LANE:(pr + 1) * LANE]
        s_new = _dot_tn(bg, xdtd_b[:, g * gw:(g + 1) * gw])
        state[:, g * gw:(g + 1) * gw] = chunk_dec_e[:, g * gw:(g + 1) * gw] * sg + s_new

    y = yacc[...] + dskip_ref[...] * xs
    u = y * _silu(z_ref[...].astype(F32))
    o = u * lax.rsqrt(jnp.mean(u * u, -1, keepdims=True) + EPS) * ng_ref[...]
    o_ref[...] = o.astype(o_ref.dtype)


def _ssd(zx, dt_raw, conv_w, conv_b, dt_bias, a_log, d_skip, norm_g):
    m = zx.shape[0]
    q = SSM_CHUNK
    nc = SEQ // q
    width = SSM_D_INNER + SSM_BC
    pad = LANE - SSM_HEADS
    dtb = jnp.pad(dt_bias, (0, pad)).reshape(1, LANE)
    alog = jnp.pad(a_log, (0, pad)).reshape(1, LANE)
    dskip_e = jnp.repeat(d_skip, SSM_HEAD_DIM).reshape(1, SSM_D_INNER)
    tril = jnp.asarray(np.tril(np.ones((q, q), np.float32)), BF16)
    e_np = np.zeros((LANE, SSM_D_INNER), np.float32)
    for h in range(SSM_HEADS):
        e_np[h, h * SSM_HEAD_DIM:(h + 1) * SSM_HEAD_DIM] = 1.0
    e = jnp.asarray(e_np, BF16)
    row = lambda b, c: (b * nc + c, 0)
    const = lambda b, c: (0, 0)
    return pl.pallas_call(
        _ssd_kernel,
        grid=(BATCH, nc),
        in_specs=[
            pl.BlockSpec((q, SSM_D_INNER), lambda b, c: (b * nc + c, COL_Z // SSM_D_INNER)),
            pl.BlockSpec((q, SSM_D_INNER), lambda b, c: (b * nc + c, COL_X // SSM_D_INNER)),
            pl.BlockSpec((q, SSM_BC), lambda b, c: (b * nc + c, COL_BC // SSM_BC)),
            pl.BlockSpec((q, LANE), row),
            pl.BlockSpec((SSM_CONV, width), const),
            pl.BlockSpec((1, width), const),
            pl.BlockSpec((1, LANE), const),
            pl.BlockSpec((1, LANE), const),
            pl.BlockSpec((1, SSM_D_INNER), const),
            pl.BlockSpec((1, SSM_D_INNER), const),
            pl.BlockSpec((q, q), const),
            pl.BlockSpec((LANE, SSM_D_INNER), const),
        ],
        out_specs=pl.BlockSpec((q, SSM_D_INNER), row),
        out_shape=jax.ShapeDtypeStruct((m, SSM_D_INNER), BF16),
        scratch_shapes=[
            pltpu.VMEM((q + 8, width), F32),
            pltpu.VMEM((SSM_STATE, SSM_D_INNER), F32),
            pltpu.VMEM((q, SSM_D_INNER), F32),
        ],
        compiler_params=_params("parallel", "arbitrary"),
        name="ssd",
    )(zx, zx, zx, dt_raw, conv_w, conv_b.reshape(1, width), dtb, alog, dskip_e,
      norm_g.reshape(1, SSM_D_INNER), tril, e)


def _t5_bucket_np(dist):
    max_exact = REL_BUCKETS // 2
    n = np.maximum(dist, 1).astype(np.float32)
    large = max_exact + (np.log(n / np.float32(max_exact)) / np.float32(math.log(REL_MAX_DIST / max_exact))
                         * np.float32(REL_BUCKETS - max_exact))
    large = np.minimum(large.astype(np.int32), REL_BUCKETS - 1)
    return np.where(dist < max_exact, dist, large).astype(np.int32)


def _bucket_table():
    i = np.arange(ATT_BLK)[:, None]
    j = np.arange(2 * ATT_BLK)[None, :]
    delta = ATT_BLK + i - j
    band = (delta >= 0) & (delta <= ATT_BLK)
    out = []
    for _, dil in ATT_PATTERNS:
        out.append(np.where(band, _t5_bucket_np(np.maximum(delta, 0) * dil), -1))
    return np.stack(out).astype(np.int32)


def _bias_kernel(tab_ref, bucket_ref, o_ref):
    g = pl.program_id(0)
    p = pl.program_id(1)
    bucket = bucket_ref[0]
    for hh in range(2):
        head = g * ATT_KV_HEADS + 2 * p + hh
        acc = jnp.full((ATT_BLK, 2 * ATT_BLK), NEG, F32)
        for b in range(REL_BUCKETS):
            acc = jnp.where(bucket == b, tab_ref[b, head], acc)
        o_ref[0, 0, hh * ATT_BLK:(hh + 1) * ATT_BLK, :] = acc


def _rel_bias(rel_table):
    buckets = jnp.asarray(_bucket_table())
    return pl.pallas_call(
        _bias_kernel,
        grid=(ATT_GROUPS, ATT_PAIRS),
        in_specs=[
            pl.BlockSpec(memory_space=pltpu.SMEM),
            pl.BlockSpec((1, ATT_BLK, 2 * ATT_BLK), lambda g, p: (g, 0, 0)),
        ],
        out_specs=pl.BlockSpec((1, 1, 2 * ATT_BLK, 2 * ATT_BLK), lambda g, p: (g, p, 0, 0)),
        out_shape=jax.ShapeDtypeStruct((ATT_GROUPS, ATT_PAIRS, 2 * ATT_BLK, 2 * ATT_BLK), F32),
        compiler_params=_params("parallel", "parallel"),
        name="rel_bias",
    )(rel_table, buckets)


def _attn_kernel(q_ref, kp_ref, kc_ref, vp_ref, vc_ref, bias_ref, o_ref, lse_ref):
    blk = ATT_BLK
    n = pl.program_id(2)
    col = lax.broadcasted_iota(jnp.int32, (1, 2 * blk), 1)
    pen = jnp.where(col < blk, NEG, 0.0) * (n == 0).astype(F32)
    lane = lax.broadcasted_iota(jnp.int32, (blk, LANE), 1)
    first = lane < ATT_HEAD_DIM
    lse_acc = jnp.zeros((blk, LANE), F32)
    for p in range(ATT_PAIRS):
        sl = slice(p * LANE, (p + 1) * LANE)
        q = q_ref[0, 0, :, sl]
        zero = jnp.zeros_like(q)
        qq = jnp.concatenate([jnp.where(first, q, zero), jnp.where(first, zero, q)], axis=0)
        kk = jnp.concatenate([kp_ref[0, 0, :, sl], kc_ref[0, 0, :, sl]], axis=0)
        vv = jnp.concatenate([vp_ref[0, 0, :, sl], vc_ref[0, 0, :, sl]], axis=0)
        s = _dot_nt(qq, kk) + bias_ref[0, p] + pen
        m = jnp.max(s, -1, keepdims=True)
        ex = jnp.exp(s - m)
        l = jnp.sum(ex, -1, keepdims=True)
        o2 = _dot(ex.astype(BF16), vv) / l
        o_ref[0, 0, :, sl] = jnp.where(first, o2[0:blk], o2[blk:2 * blk]).astype(o_ref.dtype)
        lse = m + jnp.log(l)
        lse_acc = jnp.where(lane == 2 * p, lse[0:blk], lse_acc)
        lse_acc = jnp.where(lane == 2 * p + 1, lse[blk:2 * blk], lse_acc)
    lse_ref[0, 0] = lse_acc


def _attention(qkv, bias, gi):
    dil = ATT_PATTERNS[gi][1]
    blk = ATT_BLK
    length = SEQ // dil
    nb = length // blk
    cur = lambda c: (lambda b, r, n: (b, r, n, c))
    prev = lambda c: (lambda b, r, n: (b, r, jnp.maximum(n - 1, 0), c))
    spec = lambda index_map: pl.BlockSpec((1, 1, blk, ATT_WIDTH), index_map)
    return pl.pallas_call(
        _attn_kernel,
        grid=(BATCH, dil, nb),
        in_specs=[
            spec(cur(0)), spec(prev(1)), spec(cur(1)), spec(prev(2)), spec(cur(2)),
            pl.BlockSpec((1, ATT_PAIRS, 2 * blk, 2 * blk), lambda b, r, n: (gi, 0, 0, 0)),
        ],
        out_specs=[
            spec(cur(0)),
            pl.BlockSpec((1, 1, blk, LANE), cur(0)),
        ],
        out_shape=[
            jax.ShapeDtypeStruct((BATCH, dil, length, ATT_WIDTH), BF16),
            jax.ShapeDtypeStruct((BATCH, dil, length, LANE), F32),
        ],
        compiler_params=_params("parallel", "parallel", "arbitrary"),
        name=f"attn_d{dil}",
    )(qkv, qkv, qkv, qkv, qkv, bias)


OUT_TM = 512


def _outproj_kernel(y_ref, o0_ref, o1_ref, o2_ref, l0_ref, l1_ref, l2_ref, e_ref, w_ref, x_ref, g_ref,
                    out_ref, os1, os2, ls1, ls2):
    def to_natural(src_ref, dst_ref, dil):
        rows = OUT_TM // dil
        for r in range(dil):
            for c in range(dst_ref.shape[0]):
                dst_ref[c, pl.ds(r, rows, stride=dil), :] = src_ref[0, r, :, c * LANE:(c + 1) * LANE].astype(F32)

    def natural(ref):
        return jnp.concatenate([ref[c] for c in range(ref.shape[0])], axis=1)

    to_natural(o1_ref, os1, ATT_PATTERNS[1][1])
    to_natural(o2_ref, os2, ATT_PATTERNS[2][1])
    to_natural(l1_ref, ls1, ATT_PATTERNS[1][1])
    to_natural(l2_ref, ls2, ATT_PATTERNS[2][1])
    l0, l1, l2 = l0_ref[0, 0], ls1[0], ls2[0]
    mx = jnp.maximum(jnp.maximum(l0, l1), l2)
    w0, w1, w2 = jnp.exp(l0 - mx), jnp.exp(l1 - mx), jnp.exp(l2 - mx)
    inv = 1.0 / (w0 + w1 + w2)
    e = e_ref[...]
    att = _expand(w0 * inv, e, 2) * o0_ref[0, 0].astype(F32)
    att = att + _expand(w1 * inv, e, 2) * natural(os1)
    att = att + _expand(w2 * inv, e, 2) * natural(os2)
    mix = _dot(y_ref[...], w_ref[0:SSM_D_INNER, :]) + _dot(att.astype(BF16), w_ref[SSM_D_INNER:, :])
    out_ref[...] = x_ref[...] + g_ref[0] * mix


def _outproj(y, outs, lses, w_out, x2, gate):
    m = x2.shape[0]
    tm = OUT_TM
    per_b = SEQ // tm
    e_np = np.zeros((LANE, ATT_WIDTH), np.float32)
    for h in range(ATT_KV_HEADS):
        e_np[h, h * ATT_HEAD_DIM:(h + 1) * ATT_HEAD_DIM] = 1.0
    e = jnp.asarray(e_np, BF16)
    row = lambda i: (i, 0)
    const = lambda i: (0, 0)
    dils = [d for _, d in ATT_PATTERNS]
    res = lambda dil, width: pl.BlockSpec((1, dil, tm // dil, width), lambda i: (i // per_b, 0, i % per_b, 0))
    return pl.pallas_call(
        _outproj_kernel,
        grid=(m // tm,),
        in_specs=[
            pl.BlockSpec((tm, SSM_D_INNER), row),
            res(dils[0], ATT_WIDTH), res(dils[1], ATT_WIDTH), res(dils[2], ATT_WIDTH),
            res(dils[0], LANE), res(dils[1], LANE), res(dils[2], LANE),
            pl.BlockSpec((LANE, ATT_WIDTH), const),
            pl.BlockSpec((SSM_D_INNER + ATT_WIDTH, D_MODEL), const),
            pl.BlockSpec((tm, D_MODEL), row),
            pl.BlockSpec((1, 1, D_MODEL), lambda i: (i // per_b, 0, 0)),
        ],
        out_specs=pl.BlockSpec((tm, D_MODEL), row),
        out_shape=jax.ShapeDtypeStruct((m, D_MODEL), F32),
        scratch_shapes=[
            pltpu.VMEM((ATT_WIDTH // LANE, tm, LANE), F32), pltpu.VMEM((ATT_WIDTH // LANE, tm, LANE), F32),
            pltpu.VMEM((1, tm, LANE), F32), pltpu.VMEM((1, tm, LANE), F32),
        ],
        compiler_params=_params("parallel"),
        name="outproj",
    )(y, outs[0], outs[1], outs[2], lses[0], lses[1], lses[2], e, w_out, x2, gate)


def _ffn_kernel(final_norm, x_ref, g_ref, sc_ref, sh_ref, gate_ref, wg_ref, wu_ref, wd_ref, *rest):
    if final_norm:
        fg_ref, o_ref, h_ref, acc_ref = rest
    else:
        o_ref, h_ref, acc_ref = rest
    j = pl.program_id(1)

    @pl.when(j == 0)
    def _():
        h_ref[...] = _norm_mod(x_ref[...], g_ref[...], sc_ref[0], sh_ref[0]).astype(BF16)
        acc_ref[...] = jnp.zeros(acc_ref.shape, F32)

    h = h_ref[...]
    a = (_silu(_dot(h, wg_ref[...])) * _dot(h, wu_ref[...])).astype(BF16)
    acc_ref[...] += _dot(a, wd_ref[...])

    @pl.when(j == pl.num_programs(1) - 1)
    def _():
        y = x_ref[...] + gate_ref[0] * acc_ref[...]
        if final_norm:
            y = y * lax.rsqrt(jnp.mean(y * y, -1, keepdims=True) + EPS) * fg_ref[...]
        o_ref[...] = y


def _ffn(x2, g, sc, sh, gate, wg, wu, wd, final_g=None):
    m = x2.shape[0]
    tm, th = 1024, 256
    per_b = SEQ // tm
    final_norm = final_g is not None
    batch = lambda i, j: (i // per_b, 0, 0)
    in_specs = [
        pl.BlockSpec((tm, D_MODEL), lambda i, j: (i, 0)),
        pl.BlockSpec((1, D_MODEL), lambda i, j: (0, 0)),
        pl.BlockSpec((1, 1, D_MODEL), batch),
        pl.BlockSpec((1, 1, D_MODEL), batch),
        pl.BlockSpec((1, 1, D_MODEL), batch),
        pl.BlockSpec((D_MODEL, th), lambda i, j: (0, j)),
        pl.BlockSpec((D_MODEL, th), lambda i, j: (0, j)),
        pl.BlockSpec((th, D_MODEL), lambda i, j: (j, 0)),
    ]
    args = [x2, g, sc, sh, gate, wg, wu, wd]
    if final_norm:
        in_specs.append(pl.BlockSpec((1, D_MODEL), lambda i, j: (0, 0)))
        args.append(final_g)
    return pl.pallas_call(
        functools.partial(_ffn_kernel, final_norm),
        grid=(m // tm, FFN_HIDDEN // th),
        in_specs=in_specs,
        out_specs=pl.BlockSpec((tm, D_MODEL), lambda i, j: (i, 0)),
        out_shape=jax.ShapeDtypeStruct((m, D_MODEL), F32),
        scratch_shapes=[pltpu.VMEM((tm, D_MODEL), BF16), pltpu.VMEM((tm, D_MODEL), F32)],
        compiler_params=_params("parallel", "arbitrary"),
        name="ffn_final" if final_norm else "ffn",
    )(*args)


def _glu_kernel(x_ref, g_ref, sc_ref, sh_ref, wa_ref, wg_ref, ba_ref, bg_ref, o_ref, h_ref):
    @pl.when(pl.program_id(1) == 0)
    def _():
        h_ref[...] = _norm_mod(x_ref[...], g_ref[...], sc_ref[0], sh_ref[0]).astype(BF16)

    h = h_ref[...]
    a = _dot(h, wa_ref[...]) + ba_ref[...]
    gt = _dot(h, wg_ref[...]) + bg_ref[...]
    o_ref[...] = (a * _sigmoid(gt)).astype(o_ref.dtype)


def _glu(x2, g, sc, sh, w1, b1):
    m = x2.shape[0]
    tm, tn = 1024, 512
    per_b = SEQ // tm
    half = D_MODEL // tn
    batch = lambda i, j: (i // per_b, 0, 0)
    b1r = b1.reshape(1, 2 * D_MODEL)
    return pl.pallas_call(
        _glu_kernel,
        grid=(m // tm, half),
        in_specs=[
            pl.BlockSpec((tm, D_MODEL), lambda i, j: (i, 0)),
            pl.BlockSpec((1, D_MODEL), lambda i, j: (0, 0)),
            pl.BlockSpec((1, 1, D_MODEL), batch),
            pl.BlockSpec((1, 1, D_MODEL), batch),
            pl.BlockSpec((D_MODEL, tn), lambda i, j: (0, j)),
            pl.BlockSpec((D_MODEL, tn), lambda i, j: (0, j + half)),
            pl.BlockSpec((1, tn), lambda i, j: (0, j)),
            pl.BlockSpec((1, tn), lambda i, j: (0, j + half)),
        ],
        out_specs=pl.BlockSpec((tm, tn), lambda i, j: (i, j)),
        out_shape=jax.ShapeDtypeStruct((m, D_MODEL), F32),
        scratch_shapes=[pltpu.VMEM((tm, D_MODEL), BF16)],
        compiler_params=_params("parallel", "arbitrary"),
        name="glu",
    )(x2, g, sc, sh, w1, w1, b1r, b1r)


def _cconv_kernel(ts, rc, u_ref, halo_ref, wdw_ref, bdw_ref, lng_ref, lnb_ref, w2_ref, b2_ref, x_ref,
                  gate_ref, o_ref, buf, conv):
    @pl.when(pl.program_id(1) == 0)
    def _():
        buf[0:CONV_HALO, :] = jnp.zeros((CONV_HALO, D_MODEL), F32)

    @pl.when(pl.program_id(1) > 0)
    def _():
        buf[0:CONV_HALO, :] = halo_ref[0]

    buf[CONV_HALO:CONV_HALO + ts, :] = u_ref[0]
    off = CONV_HALO - (CONV_WIDTH - 1)
    wrows = rc + CONV_HALO

    def chunk(c, carry):
        r0 = pl.multiple_of(c * rc, rc)
        for lc in range(D_MODEL // LANE):
            lanes = slice(lc * LANE, (lc + 1) * LANE)
            win = buf[pl.ds(r0, wrows), lanes]
            acc = jnp.broadcast_to(bdw_ref[:, lanes], (rc, LANE))
            for s in range(8):
                sh = win if s == 0 else pltpu.roll(win, wrows - s, axis=0)
                for a in range(wrows // 8):
                    k = 8 * a + s - off
                    if 0 <= k < CONV_WIDTH:
                        acc = acc + wdw_ref[k:k + 1, lanes] * sh[8 * a:8 * a + rc, :]
            conv[pl.ds(r0, rc), lanes] = acc
        return carry

    lax.fori_loop(0, ts // rc, chunk, 0)
    u = conv[...]
    mu = jnp.mean(u, -1, keepdims=True)
    d = u - mu
    var = jnp.mean(d * d, -1, keepdims=True)
    v = _silu(d * lax.rsqrt(var + EPS) * lng_ref[...] + lnb_ref[...])
    mix = _dot(v.astype(BF16), w2_ref[...]) + b2_ref[...]
    o_ref[0] = x_ref[0] + gate_ref[0] * mix


def _cconv(u, w_dw, b_dw, ln_g, ln_b, w2, b2, x2, gate):
    ts, rc = 512, 64
    nt = SEQ // ts
    hb = ts // CONV_HALO
    u3 = u.reshape(BATCH, SEQ, D_MODEL)
    x3 = x2.reshape(BATCH, SEQ, D_MODEL)
    wdw = jnp.pad(w_dw, ((0, CONV_HALO - CONV_WIDTH), (0, 0)))
    vec = lambda v: v.reshape(1, D_MODEL)
    const = lambda b, i: (0, 0)
    out = pl.pallas_call(
        functools.partial(_cconv_kernel, ts, rc),
        grid=(BATCH, nt),
        in_specs=[
            pl.BlockSpec((1, ts, D_MODEL), lambda b, i: (b, i, 0)),
            pl.BlockSpec((1, CONV_HALO, D_MODEL), lambda b, i: (b, jnp.maximum(i * hb - 1, 0), 0)),
            pl.BlockSpec((CONV_HALO, D_MODEL), const),
            pl.BlockSpec((1, D_MODEL), const),
            pl.BlockSpec((1, D_MODEL), const),
            pl.BlockSpec((1, D_MODEL), const),
            pl.BlockSpec((D_MODEL, D_MODEL), const),
            pl.BlockSpec((1, D_MODEL), const),
            pl.BlockSpec((1, ts, D_MODEL), lambda b, i: (b, i, 0)),
            pl.BlockSpec((1, 1, D_MODEL), lambda b, i: (b, 0, 0)),
        ],
        out_specs=pl.BlockSpec((1, ts, D_MODEL), lambda b, i: (b, i, 0)),
        out_shape=jax.ShapeDtypeStruct((BATCH, SEQ, D_MODEL), F32),
        scratch_shapes=[pltpu.VMEM((CONV_HALO + ts, D_MODEL), F32), pltpu.VMEM((ts, D_MODEL), F32)],
        compiler_params=_params("parallel", "arbitrary"),
        name="cconv",
    )(u3, u3, wdw, vec(b_dw), vec(ln_g), vec(ln_b), w2, vec(b2), x3, gate)
    return out.reshape(BATCH * SEQ, D_MODEL)


def kernel(x, c, ada_w, ada_b, norm_mix_g, norm_ffn_g, hy_w_in, hy_conv_w, hy_conv_b, hy_dt_bias, hy_a_log,
           hy_d_skip, hy_ssm_norm_g, hy_w_out, rel_table, cv_w_pw1, cv_b_pw1, cv_w_dw, cv_b_dw, cv_ln_g,
           cv_ln_b, cv_w_pw2, cv_b_pw2, ffn_w_gate, ffn_w_up, ffn_w_down, final_norm_g):
    assert x.shape == (BATCH, SEQ, D_MODEL) and c.shape == (BATCH, D_MODEL)
    m = BATCH * SEQ
    x2 = x.reshape(m, D_MODEL)
    mod = _modulation(c, ada_w, ada_b)

    def mods(i):
        parts = jnp.split(mod[i, :BATCH], N_MOD, axis=-1)
        return [p.reshape(BATCH, 1, D_MODEL) for p in parts]

    vec = lambda v: v.reshape(1, D_MODEL)

    sh1, sc1, g1, sh2, sc2, g2 = mods(0)
    w_in = hy_w_in[0]
    s_z, s_xbc, s_dt = SSM_D_INNER, SSM_D_INNER + SSM_D_INNER + SSM_BC, SSM_D_INNER + SSM_D_INNER + SSM_BC + SSM_HEADS
    s_q = s_dt + ATT_GROUPS * ATT_WIDTH
    w_main = jnp.concatenate([
        w_in[:, :s_xbc],
        w_in[:, s_dt:s_q] * (ATT_HEAD_DIM ** -0.5),
        w_in[:, s_q:],
    ], axis=1).astype(BF16)
    w_dt = jnp.pad(w_in[:, s_xbc:s_dt], ((0, 0), (0, LANE - SSM_HEADS))).astype(BF16)
    zx, qkv0, qkv1, qkv2, dt_raw = _inproj(x2, vec(norm_mix_g[0]), sc1, sh1, w_main, w_dt)
    y = _ssd(zx, dt_raw, hy_conv_w[0], hy_conv_b[0], hy_dt_bias[0], hy_a_log[0], hy_d_skip[0],
             hy_ssm_norm_g[0])
    bias = _rel_bias(rel_table)
    outs, lses = [], []
    for gi, qkv in enumerate((qkv0, qkv1, qkv2)):
        o, lse = _attention(qkv, bias, gi)
        outs.append(o)
        lses.append(lse)
    x2 = _outproj(y, outs, lses, hy_w_out[0].astype(BF16), x2, g1)
    x2 = _ffn(x2, vec(norm_ffn_g[0]), sc2, sh2, g2, ffn_w_gate[0].astype(BF16), ffn_w_up[0].astype(BF16),
              ffn_w_down[0].astype(BF16))

    sh1, sc1, g1, sh2, sc2, g2 = mods(1)
    u = _glu(x2, vec(norm_mix_g[1]), sc1, sh1, cv_w_pw1[0].astype(BF16), cv_b_pw1[0])
    x2 = _cconv(u, cv_w_dw[0], cv_b_dw[0], cv_ln_g[0], cv_ln_b[0], cv_w_pw2[0].astype(BF16), cv_b_pw2[0],
                x2, g1)
    x2 = _ffn(x2, vec(norm_ffn_g[1]), sc2, sh2, g2, ffn_w_gate[1].astype(BF16), ffn_w_up[1].astype(BF16),
              ffn_w_down[1].astype(BF16), final_g=vec(final_norm_g))
    return x2.reshape(BATCH, SEQ, D_MODEL)
```
